```python
import jax, jax.numpy as jnp
from jax import lax
import numpy as np

D_MODEL = 1024
BATCH = 2
SEQ = 8192
DEPTH = 2

D_PLE = 256
SB_HEADS = 8
SB_HEAD_DIM = 64
SB_WIDTH = SB_HEADS * SB_HEAD_DIM
MLA_HEADS = 4
MLA_NOPE_DIM = 128
MLA_ROPE_DIM = 64
MLA_QK_DIM = MLA_NOPE_DIM + MLA_ROPE_DIM
MLA_V_DIM = 128
MLA_WIDTH = MLA_HEADS * MLA_V_DIM
Q_LORA = 256
KV_LORA = 128
D_MIX = SB_WIDTH + MLA_WIDTH
IN_COLS = 3 * SB_WIDTH + Q_LORA + KV_LORA + MLA_ROPE_DIM
D_FF = 2816
ROPE_THETA = 10000.0
EPS = 1e-6
Q_BLOCK = 128

kernel_name = "hybrid_stickbreak_mla_macaron_ple"


def rms_norm(x, g):
    xf = x.astype(jnp.float32)
    y = xf * lax.rsqrt(jnp.mean(xf * xf, axis=-1, keepdims=True) + EPS)
    return (y * g.astype(jnp.float32)).astype(x.dtype)


def swiglu(x, w_gate, w_up, w_down):
    return (jax.nn.silu(x @ w_gate) * (x @ w_up)) @ w_down


def rope_cos_sin(seq_len):
    half = MLA_ROPE_DIM // 2
    inv_freq = ROPE_THETA ** (-jnp.arange(half, dtype=jnp.float32) / half)
    ang = jnp.arange(seq_len, dtype=jnp.float32)[:, None] * inv_freq[None, :]
    return jnp.cos(ang), jnp.sin(ang)


def apply_rope(x, cos, sin):
    xf = x.astype(jnp.float32)
    x1, x2 = jnp.split(xf, 2, axis=-1)
    out = jnp.concatenate([x1 * cos - x2 * sin, x2 * cos + x1 * sin], axis=-1)
    return out.astype(x.dtype)


def to_heads(t, n_heads):
    b, s, _ = t.shape
    return t.reshape(b, s, n_heads, -1).transpose(0, 2, 1, 3)


def from_heads(t):
    b, h, s, d = t.shape
    return t.transpose(0, 2, 1, 3).reshape(b, s, h * d)


def query_blocks(t):
    b, h, s, d = t.shape
    return t.reshape(b, h, s // Q_BLOCK, Q_BLOCK, d).transpose(2, 0, 1, 3, 4)


def unblock(t):
    nb, b, h, qb, d = t.shape
    return t.transpose(1, 2, 0, 3, 4).reshape(b, h, nb * qb, d)


def stick_breaking_attention(q, k, v):
    s_len, d = q.shape[2], q.shape[3]
    scale = d ** -0.5
    key_pos = jnp.arange(s_len)

    def block(args):
        q_blk, i = args
        q_pos = i * Q_BLOCK + jnp.arange(Q_BLOCK)
        z = jnp.einsum('bhqd,bhkd->bhqk', q_blk, k,
                       preferred_element_type=jnp.float32) * scale
        past = key_pos[None, :] < q_pos[:, None]
        log_one_minus = jnp.where(past, jax.nn.log_sigmoid(-z), 0.0)
        tail = lax.cumsum(log_one_minus, axis=3, reverse=True) - log_one_minus
        a = jnp.where(past, jnp.exp(jax.nn.log_sigmoid(z) + tail), 0.0)
        return jnp.einsum('bhqk,bhkd->bhqd', a.astype(v.dtype), v)

    out = lax.map(block, (query_blocks(q), jnp.arange(s_len // Q_BLOCK)))
    return unblock(out)


def mla_attention(q_nope, q_rope, k_nope, k_rope, v):
    s_len = q_nope.shape[2]
    scale = MLA_QK_DIM ** -0.5
    key_pos = jnp.arange(s_len)
    neg = jnp.finfo(jnp.float32).min

    def block(args):
        qn, qr, i = args
        q_pos = i * Q_BLOCK + jnp.arange(Q_BLOCK)
        s = (jnp.einsum('bhqd,bhkd->bhqk', qn, k_nope, preferred_element_type=jnp.float32)
             + jnp.einsum('bhqr,bkr->bhqk', qr, k_rope, preferred_element_type=jnp.float32)) * scale
        causal = key_pos[None, :] <= q_pos[:, None]
        probs = jax.nn.softmax(jnp.where(causal, s, neg), axis=-1)
        return jnp.einsum('bhqk,bhkd->bhqd', probs.astype(v.dtype), v)

    out = lax.map(block, (query_blocks(q_nope), query_blocks(q_rope),
                          jnp.arange(s_len // Q_BLOCK)))
    return unblock(out)


def hybrid_layer(h, p_i, cos, sin,
                 ffn1_norm, ffn1_w_gate, ffn1_w_up, ffn1_w_down,
                 mix_norm, w_in, q_lat_norm, w_uq, kv_lat_norm, w_ukv,
                 sb_out_norm, mla_out_norm, w_out,
                 ffn2_norm, ffn2_w_gate, ffn2_w_up, ffn2_w_down,
                 ple_norm, w_ple_gate, w_ple_proj):
    b, s, _ = h.shape
    h = h + 0.5 * swiglu(rms_norm(h, ffn1_norm), ffn1_w_gate, ffn1_w_up, ffn1_w_down)

    u = rms_norm(h, mix_norm)
    proj = u @ w_in
    o1 = SB_WIDTH
    o2 = 2 * SB_WIDTH
    o3 = 3 * SB_WIDTH
    o4 = o3 + Q_LORA
    o5 = o4 + KV_LORA
    sb_q, sb_k, sb_v, c_q, c_kv, k_rope = jnp.split(proj, [o1, o2, o3, o4, o5], axis=-1)

    sb_out = stick_breaking_attention(to_heads(sb_q, SB_HEADS), to_heads(sb_k, SB_HEADS),
                                      to_heads(sb_v, SB_HEADS))
    sb_out = rms_norm(from_heads(sb_out), sb_out_norm)

    q = (rms_norm(c_q, q_lat_norm) @ w_uq).reshape(b, s, MLA_HEADS, MLA_QK_DIM)
    q_nope, q_rope = q[..., :MLA_NOPE_DIM], q[..., MLA_NOPE_DIM:]
    q_rope = apply_rope(q_rope, cos[:, None, :], sin[:, None, :])
    kv = (rms_norm(c_kv, kv_lat_norm) @ w_ukv).reshape(b, s, MLA_HEADS, MLA_NOPE_DIM + MLA_V_DIM)
    k_nope, v = kv[..., :MLA_NOPE_DIM], kv[..., MLA_NOPE_DIM:]
    k_rope = apply_rope(k_rope, cos, sin)
    mla_out = mla_attention(q_nope.transpose(0, 2, 1, 3), q_rope.transpose(0, 2, 1, 3),
                            k_nope.transpose(0, 2, 1, 3), k_rope, v.transpose(0, 2, 1, 3))
    mla_out = rms_norm(from_heads(mla_out), mla_out_norm)

    mixed = jnp.concatenate([sb_out, mla_out], axis=-1)
    h = h + mixed @ w_out

    h = h + 0.5 * swiglu(rms_norm(h, ffn2_norm), ffn2_w_gate, ffn2_w_up, ffn2_w_down)

    gate = jax.nn.sigmoid(rms_norm(h, ple_norm) @ w_ple_gate)
    h = h + gate * (p_i @ w_ple_proj)
    return h


def setup_inputs(seed: int = 0) -> dict:
    key = jax.random.key(seed)
    ks = iter(jax.random.split(key, 32))

    def w(shape, fan_in):
        return jax.random.normal(next(ks), shape, jnp.float32) * (fan_in ** -0.5)

    def gain(shape):
        return 1.0 + 0.02 * jax.random.normal(next(ks), shape, jnp.float32)

    L = DEPTH
    return {
        "x": jax.random.normal(next(ks), (BATCH, SEQ, D_MODEL), jnp.float32),
        "p": jax.random.normal(next(ks), (DEPTH, BATCH, SEQ, D_PLE), jnp.float32),
        "ffn1_norm": gain((L, D_MODEL)),
        "ffn1_w_gate": w((L, D_MODEL, D_FF), D_MODEL),
        "ffn1_w_up": w((L, D_MODEL, D_FF), D_MODEL),
        "ffn1_w_down": w((L, D_FF, D_MODEL), D_FF),
        "mix_norm": gain((L, D_MODEL)),
        "w_in": w((L, D_MODEL, IN_COLS), D_MODEL),
        "q_lat_norm": gain((L, Q_LORA)),
        "w_uq": w((L, Q_LORA, MLA_HEADS * MLA_QK_DIM), Q_LORA),
        "kv_lat_norm": gain((L, KV_LORA)),
        "w_ukv": w((L, KV_LORA, MLA_HEADS * (MLA_NOPE_DIM + MLA_V_DIM)), KV_LORA),
        "sb_out_norm": gain((L, SB_WIDTH)),
        "mla_out_norm": gain((L, MLA_WIDTH)),
        "w_out": w((L, D_MIX, D_MODEL), D_MIX),
        "ffn2_norm": gain((L, D_MODEL)),
        "ffn2_w_gate": w((L, D_MODEL, D_FF), D_MODEL),
        "ffn2_w_up": w((L, D_MODEL, D_FF), D_MODEL),
        "ffn2_w_down": w((L, D_FF, D_MODEL), D_FF),
        "ple_norm": gain((L, D_MODEL)),
        "w_ple_gate": w((L, D_MODEL, D_MODEL), D_MODEL),
        "w_ple_proj": w((L, D_PLE, D_MODEL), D_PLE),
        "final_norm": gain((D_MODEL,)),
    }


def reference(x, p, ffn1_norm, ffn1_w_gate, ffn1_w_up, ffn1_w_down,
              mix_norm, w_in, q_lat_norm, w_uq, kv_lat_norm, w_ukv,
              sb_out_norm, mla_out_norm, w_out,
              ffn2_norm, ffn2_w_gate, ffn2_w_up, ffn2_w_down,
              ple_norm, w_ple_gate, w_ple_proj, final_norm):
    cos, sin = rope_cos_sin(x.shape[1])
    h = x
    for i in range(DEPTH):
        h = hybrid_layer(h, p[i], cos, sin,
                         ffn1_norm[i], ffn1_w_gate[i], ffn1_w_up[i], ffn1_w_down[i],
                         mix_norm[i], w_in[i], q_lat_norm[i], w_uq[i], kv_lat_norm[i], w_ukv[i],
                         sb_out_norm[i], mla_out_norm[i], w_out[i],
                         ffn2_norm[i], ffn2_w_gate[i], ffn2_w_up[i], ffn2_w_down[i],
                         ple_norm[i], w_ple_gate[i], w_ple_proj[i])
    return rms_norm(h, final_norm)
```

```python
import functools

import jax
import jax.numpy as jnp
from jax import lax
from jax.experimental import pallas as pl
from jax.experimental.pallas import tpu as pltpu

D_MODEL = 1024
D_PLE = 256
SB_HEADS = 8
SB_HEAD_DIM = 64
SB_WIDTH = SB_HEADS * SB_HEAD_DIM
MLA_HEADS = 4
MLA_NOPE_DIM = 128
MLA_ROPE_DIM = 64
MLA_QK_DIM = MLA_NOPE_DIM + MLA_ROPE_DIM
MLA_V_DIM = 128
MLA_WIDTH = MLA_HEADS * MLA_V_DIM
Q_LORA = 256
KV_LORA = 128
D_FF = 2816
ROPE_THETA = 10000.0
EPS = 1e-6

LANES = 128
MXU_DIM = 256
VMEM_LIMIT_BYTES = 56 * 1024 * 1024

FF_CHUNK = MXU_DIM
N_FF_CHUNKS = D_FF // FF_CHUNK
TOK_TILE = 512
ATT_TILE = 256
MLA_SLAB = 2 * LANES
SB_DEAD_LOG = -105.0

_BF = jnp.bfloat16
_F32 = jnp.float32
_NT = (((1,), (1,)), ((), ()))


def _rms(x, g):
    return x * lax.rsqrt(jnp.mean(x * x, axis=-1, keepdims=True) + EPS) * g


def _params(*sem):
    return pltpu.CompilerParams(dimension_semantics=sem, vmem_limit_bytes=VMEM_LIMIT_BYTES)


def _resident(shape):
    zeros = (0,) * len(shape)
    return pl.BlockSpec(shape, lambda *_: zeros, pipeline_mode=pl.Buffered(1))


def _ffn_kernel(x_ref, g_ref, wg_ref, wu_ref, wd_ref, o_ref):
    x = x_ref[...]
    xn = _rms(x, g_ref[...]).astype(_BF)
    acc = jnp.zeros(x.shape, _F32)
    for c in range(N_FF_CHUNKS):
        gate = jnp.dot(xn, wg_ref[c], preferred_element_type=_F32)
        up = jnp.dot(xn, wu_ref[c], preferred_element_type=_F32)
        hid = (gate * jax.nn.sigmoid(gate) * up).astype(_BF)
        acc = acc + jnp.dot(hid, wd_ref[c], preferred_element_type=_F32)
    o_ref[...] = x + 0.5 * acc


def _ffn(x, g, wg, wu, wd):
    t = x.shape[0]
    row = pl.BlockSpec((TOK_TILE, D_MODEL), lambda i: (i, 0))
    return pl.pallas_call(
        _ffn_kernel,
        grid=(t // TOK_TILE,),
        in_specs=[row, _resident(g.shape), _resident(wg.shape), _resident(wu.shape),
                  _resident(wd.shape)],
        out_specs=row,
        out_shape=jax.ShapeDtypeStruct(x.shape, _F32),
        compiler_params=_params("parallel"),
        name="ffn",
    )(x, g, wg, wu, wd)


def _proj_kernel(h_ref, g_ref, wq_ref, wk_ref, wvt_ref, wlat_ref, gq_ref, gkv_ref,
                 wuqa_ref, wuqb_ref, wukv_ref, cos_ref, sin_ref,
                 sbq_ref, sbk_ref, sbvt_ref, mq_ref, mk_ref, mv_ref):
    u = _rms(h_ref[...], g_ref[...]).astype(_BF)
    sbq_ref[...] = jnp.dot(u, wq_ref[...], preferred_element_type=_F32).astype(_BF)
    sbk_ref[...] = jnp.dot(u, wk_ref[...], preferred_element_type=_F32).astype(_BF)
    sbvt_ref[0, 0] = lax.dot_general(wvt_ref[...], u, _NT,
                                     preferred_element_type=_F32).astype(_BF)

    lat = jnp.dot(u, wlat_ref[...], preferred_element_type=_F32)
    cos = cos_ref[...]
    sin = sin_ref[...]
    cq = _rms(lat[:, :Q_LORA], gq_ref[...]).astype(_BF)
    ckv = _rms(lat[:, Q_LORA:Q_LORA + KV_LORA], gkv_ref[...]).astype(_BF)
    o_kr = Q_LORA + KV_LORA
    k_rope = lat[:, o_kr:o_kr + LANES] * cos + lat[:, o_kr + LANES:o_kr + 2 * LANES] * sin
    k_rope = k_rope.astype(_BF)

    qa = jnp.dot(cq, wuqa_ref[...], preferred_element_type=_F32)
    qb = jnp.dot(cq, wuqb_ref[...], preferred_element_type=_F32)
    kv = jnp.dot(ckv, wukv_ref[...], preferred_element_type=_F32)
    scale = MLA_QK_DIM ** -0.5
    for hd in range(MLA_HEADS):
        o = hd * MLA_SLAB
        mq_ref[:, o:o + LANES] = (qa[:, o:o + LANES] * scale).astype(_BF)
        roped = qa[:, o + LANES:o + 2 * LANES] * cos + qb[:, hd * LANES:(hd + 1) * LANES] * sin
        mq_ref[:, o + LANES:o + 2 * LANES] = (roped * scale).astype(_BF)
        mk_ref[:, o:o + LANES] = kv[:, hd * LANES:(hd + 1) * LANES].astype(_BF)
        mk_ref[:, o + LANES:o + 2 * LANES] = k_rope
    mv_ref[...] = kv[:, MLA_HEADS * LANES:].astype(_BF)


def _proj(h, g, w, cos2, sin2, batch, seq):
    t = batch * seq
    tm = ATT_TILE
    nblk = seq // tm
    row = lambda n: pl.BlockSpec((tm, n), lambda i: (i, 0))
    outs = (
        jax.ShapeDtypeStruct((t, SB_WIDTH), _BF),
        jax.ShapeDtypeStruct((t, SB_WIDTH), _BF),
        jax.ShapeDtypeStruct((batch, nblk, SB_WIDTH, tm), _BF),
        jax.ShapeDtypeStruct((t, MLA_HEADS * MLA_SLAB), _BF),
        jax.ShapeDtypeStruct((t, MLA_HEADS * MLA_SLAB), _BF),
        jax.ShapeDtypeStruct((t, MLA_WIDTH), _BF),
    )
    weights = (g, w["wq"], w["wk"], w["wvt"], w["wlat"], w["gq"], w["gkv"],
               w["wuqa"], w["wuqb"], w["wukv"])
    rope_spec = pl.BlockSpec((tm, LANES), lambda i: (i % nblk, 0))
    return pl.pallas_call(
        _proj_kernel,
        grid=(t // tm,),
        in_specs=[row(D_MODEL)] + [_resident(a.shape) for a in weights] + [rope_spec, rope_spec],
        out_specs=(row(SB_WIDTH), row(SB_WIDTH),
                   pl.BlockSpec((1, 1, SB_WIDTH, tm), lambda i: (i // nblk, i % nblk, 0, 0)),
                   row(MLA_HEADS * MLA_SLAB), row(MLA_HEADS * MLA_SLAB), row(MLA_WIDTH)),
        out_shape=outs,
        compiler_params=_params("parallel"),
        name="mixer_proj",
    )(h, *weights, cos2, sin2)


def _sb_tile(kb, vb, tneg, qh, carry, acc, valid):
    z = lax.dot_general(kb, qh, _NT, preferred_element_type=_F32)
    sp = jnp.maximum(z, 0.0) + jnp.log(1.0 + jnp.exp(-jnp.abs(z)))
    if valid is not None:
        sp = jnp.where(valid, sp, 0.0)
    tail = jnp.dot(tneg, sp.astype(_BF), preferred_element_type=_F32)
    a = jnp.exp((z - sp) + tail + carry)
    if valid is not None:
        a = jnp.where(valid, a, 0.0)
    acc = acc + jnp.dot(vb, a.astype(_BF), preferred_element_type=_F32)
    carry = carry + tail[0:1, :] - sp[0:1, :]
    return carry, acc


def _sb_kernel(q_ref, k_ref, vt_ref, tneg_ref, o_ref):
    tb = ATT_TILE
    i = pl.program_id(2)
    q = q_ref[...]
    lane = lax.broadcasted_iota(jnp.int32, q.shape, 1)
    zero = jnp.zeros_like(q)
    qs = (jnp.where(lane < SB_HEAD_DIM, q, zero), jnp.where(lane >= SB_HEAD_DIM, q, zero))
    tneg = tneg_ref[...]
    key_row = lax.broadcasted_iota(jnp.int32, (tb, tb), 0)
    qry_col = lax.broadcasted_iota(jnp.int32, (tb, tb), 1)
    valid = key_row < qry_col

    def both_heads(j, carries, accs, mask):
        kb = k_ref[pl.ds(pl.multiple_of(j * tb, tb), tb), :]
        vb = vt_ref[0, j]
        out = [_sb_tile(kb, vb[hh * SB_HEAD_DIM:(hh + 1) * SB_HEAD_DIM], tneg, qs[hh],
                        carries[hh], accs[hh], mask) for hh in range(2)]
        return tuple(o[0] for o in out), tuple(o[1] for o in out)

    zc = jnp.zeros((1, tb), _F32)
    za = jnp.zeros((SB_HEAD_DIM, tb), _F32)
    carries, accs = both_heads(i, (zc, zc), (za, za), valid)

    def alive(carries):
        top = jnp.max(jnp.maximum(carries[0], carries[1]))
        return (top > SB_DEAD_LOG).astype(jnp.int32)

    def cond(state):
        j, live, _, _ = state
        return jnp.logical_and(j >= 0, live > 0)

    def body(state):
        j, _, carries, accs = state
        carries, accs = both_heads(j, carries, accs, None)
        return j - 1, alive(carries), carries, accs

    _, _, _, accs = lax.while_loop(cond, body, (i - 1, alive(carries), carries, accs))
    o_ref[...] = jnp.concatenate(accs, axis=0).T


def _sb_attention(sbq, sbk, sbvt, tneg, batch, seq):
    tb = ATT_TILE
    nblk = seq // tb
    pairs = SB_HEADS // 2
    return pl.pallas_call(
        _sb_kernel,
        grid=(batch, pairs, nblk),
        in_specs=[
            pl.BlockSpec((tb, LANES), lambda b, p, i: (b * nblk + i, p)),
            pl.BlockSpec((seq, LANES), lambda b, p, i: (b, p)),
            pl.BlockSpec((1, nblk, LANES, tb), lambda b, p, i: (b, 0, p, 0)),
            _resident(tneg.shape),
        ],
        out_specs=pl.BlockSpec((tb, LANES), lambda b, p, i: (b * nblk + i, p)),
        out_shape=jax.ShapeDtypeStruct((batch * seq, SB_WIDTH), _F32),
        compiler_params=_params("parallel", "parallel", "arbitrary"),
        name="sb_attention",
    )(sbq, sbk, sbvt, tneg)


def _mla_tile(q, kb, vb, m, l, acc, valid):
    s = lax.dot_general(q, kb, _NT, preferred_element_type=_F32)
    if valid is not None:
        s = jnp.where(valid, s, -jnp.inf)
    m_new = jnp.maximum(m, jnp.max(s, axis=1, keepdims=True))
    alpha = jnp.exp(m - m_new)
    p = jnp.exp(s - m_new)
    l = alpha * l + jnp.sum(p, axis=1, keepdims=True)
    acc = alpha * acc + jnp.dot(p.astype(_BF), vb, preferred_element_type=_F32)
    return m_new, l, acc


def _mla_kernel(q_ref, k_ref, v_ref, o_ref):
    tb = ATT_TILE
    i = pl.program_id(2)
    q = q_ref[...]

    def block(j):
        rows = pl.ds(pl.multiple_of(j * tb, tb), tb)
        return k_ref[rows, :], v_ref[rows, :]

    def body(j, state):
        kb, vb = block(j)
        return _mla_tile(q, kb, vb, *state, None)

    init = (jnp.full((tb, 1), -1e30, _F32), jnp.zeros((tb, 1), _F32),
            jnp.zeros((tb, MLA_V_DIM), _F32))
    state = lax.fori_loop(0, i, body, init)
    qry_row = lax.broadcasted_iota(jnp.int32, (tb, tb), 0)
    key_col = lax.broadcasted_iota(jnp.int32, (tb, tb), 1)
    kb, vb = block(i)
    _, l, acc = _mla_tile(q, kb, vb, *state, key_col <= qry_row)
    o_ref[...] = acc / l


def _mla_attention(mq, mk, mv, batch, seq):
    tb = ATT_TILE
    nblk = seq // tb
    return pl.pallas_call(
        _mla_kernel,
        grid=(batch, MLA_HEADS, nblk),
        in_specs=[
            pl.BlockSpec((tb, MLA_SLAB), lambda b, h, i: (b * nblk + i, h)),
            pl.BlockSpec((seq, MLA_SLAB), lambda b, h, i: (b, h)),
            pl.BlockSpec((seq, MLA_V_DIM), lambda b, h, i: (b, h)),
        ],
        out_specs=pl.BlockSpec((tb, MLA_V_DIM), lambda b, h, i: (b * nblk + i, h)),
        out_shape=jax.ShapeDtypeStruct((batch * seq, MLA_WIDTH), _F32),
        compiler_params=_params("parallel", "parallel", "arbitrary"),
        name="mla_attention",
    )(mq, mk, mv)


def _out_kernel(h_ref, sb_ref, mla_ref, gsb_ref, gmla_ref, wsb_ref, wmla_ref, o_ref):
    sb = _rms(sb_ref[...], gsb_ref[...]).astype(_BF)
    ml = _rms(mla_ref[...], gmla_ref[...]).astype(_BF)
    y = jnp.dot(sb, wsb_ref[...], preferred_element_type=_F32)
    y = y + jnp.dot(ml, wmla_ref[...], preferred_element_type=_F32)
    o_ref[...] = h_ref[...] + y


def _out_proj(h, sb, mla, gsb, gmla, wsb, wmla):
    t = h.shape[0]
    row = lambda n: pl.BlockSpec((TOK_TILE, n), lambda i: (i, 0))
    return pl.pallas_call(
        _out_kernel,
        grid=(t // TOK_TILE,),
        in_specs=[row(D_MODEL), row(SB_WIDTH), row(MLA_WIDTH), _resident(gsb.shape),
                  _resident(gmla.shape), _resident(wsb.shape), _resident(wmla.shape)],
        out_specs=row(D_MODEL),
        out_shape=jax.ShapeDtypeStruct(h.shape, _F32),
        compiler_params=_params("parallel"),
        name="out_proj",
    )(h, sb, mla, gsb, gmla, wsb, wmla)


def _ple_kernel(h_ref, p_ref, g_ref, wg_ref, wp_ref, gf_ref, o_ref, *, final_norm):
    h = h_ref[...]
    hn = _rms(h, g_ref[...]).astype(_BF)
    gate = jax.nn.sigmoid(jnp.dot(hn, wg_ref[...], preferred_element_type=_F32))
    emb = jnp.dot(p_ref[...].astype(_BF), wp_ref[...], preferred_element_type=_F32)
    out = h + gate * emb
    if final_norm:
        out = _rms(out, gf_ref[...])
    o_ref[...] = out


def _ple(h, p, g, wg, wp, gf, final_norm):
    t = h.shape[0]
    row = lambda n: pl.BlockSpec((TOK_TILE, n), lambda i: (i, 0))
    return pl.pallas_call(
        functools.partial(_ple_kernel, final_norm=final_norm),
        grid=(t // TOK_TILE,),
        in_specs=[row(D_MODEL), row(D_PLE), _resident(g.shape), _resident(wg.shape),
                  _resident(wp.shape), _resident(gf.shape)],
        out_specs=row(D_MODEL),
        out_shape=jax.ShapeDtypeStruct(h.shape, _F32),
        compiler_params=_params("parallel"),
        name="ple",
    )(h, p, g, wg, wp, gf)


def _swap_cols(w):
    half = w.shape[-1] // 2
    return jnp.concatenate([-w[..., half:], w[..., :half]], axis=-1)


def _pad_cols(w, n):
    return jnp.pad(w, ((0, 0), (0, n - w.shape[-1])))


def _ffn_weights(w_gate, w_up, w_down):
    split = lambda w: w.reshape(D_MODEL, N_FF_CHUNKS, FF_CHUNK).transpose(1, 0, 2).astype(_BF)
    return split(w_gate), split(w_up), w_down.reshape(N_FF_CHUNKS, FF_CHUNK, D_MODEL).astype(_BF)


def _mixer_weights(w_in, q_lat_norm, w_uq, kv_lat_norm, w_ukv):
    o1, o2, o3 = SB_WIDTH, 2 * SB_WIDTH, 3 * SB_WIDTH
    o4 = o3 + Q_LORA
    o5 = o4 + KV_LORA
    w_kr = w_in[:, o5:]
    wlat = jnp.concatenate([w_in[:, o3:o5], _pad_cols(w_kr, LANES),
                            _pad_cols(_swap_cols(w_kr), LANES)], axis=1)
    uq = w_uq.reshape(Q_LORA, MLA_HEADS, MLA_QK_DIM)
    rope_w = uq[:, :, MLA_NOPE_DIM:]
    wuqa = jnp.pad(uq, ((0, 0), (0, 0), (0, MLA_SLAB - MLA_QK_DIM)))
    wuqb = jnp.pad(_swap_cols(rope_w), ((0, 0), (0, 0), (0, LANES - MLA_ROPE_DIM)))
    ukv = w_ukv.reshape(KV_LORA, MLA_HEADS, MLA_NOPE_DIM + MLA_V_DIM)
    wukv = jnp.concatenate([ukv[:, :, :MLA_NOPE_DIM].reshape(KV_LORA, -1),
                            ukv[:, :, MLA_NOPE_DIM:].reshape(KV_LORA, -1)], axis=1)
    return {
        "wq": (w_in[:, :o1] * SB_HEAD_DIM ** -0.5).astype(_BF),
        "wk": w_in[:, o1:o2].astype(_BF),
        "wvt": w_in[:, o2:o3].T.astype(_BF),
        "wlat": wlat.astype(_BF),
        "gq": q_lat_norm.reshape(1, -1),
        "gkv": kv_lat_norm.reshape(1, -1),
        "wuqa": wuqa.reshape(Q_LORA, -1).astype(_BF),
        "wuqb": wuqb.reshape(Q_LORA, -1).astype(_BF),
        "wukv": wukv.astype(_BF),
    }


def _rope_tables(seq):
    half = MLA_ROPE_DIM // 2
    inv_freq = ROPE_THETA ** (-jnp.arange(half, dtype=_F32) / half)
    ang = jnp.arange(seq, dtype=_F32)[:, None] * inv_freq[None, :]
    cos, sin = jnp.cos(ang), jnp.sin(ang)
    pad = jnp.zeros((seq, LANES - MLA_ROPE_DIM), _F32)
    return (jnp.concatenate([cos, cos, pad], axis=1), jnp.concatenate([sin, sin, pad], axis=1))


def kernel(x, p, ffn1_norm, ffn1_w_gate, ffn1_w_up, ffn1_w_down, mix_norm, w_in, q_lat_norm, w_uq, kv_lat_norm, w_ukv, sb_out_norm, mla_out_norm, w_out, ffn2_norm, ffn2_w_gate, ffn2_w_up, ffn2_w_down, ple_norm, w_ple_gate, w_ple_proj, final_norm):
    batch, seq, _ = x.shape
    depth = p.shape[0]
    assert seq % ATT_TILE == 0 and (batch * seq) % TOK_TILE == 0
    t = batch * seq
    cos2, sin2 = _rope_tables(seq)
    idx = jnp.arange(ATT_TILE)
    tneg = jnp.where(idx[None, :] > idx[:, None], -1.0, 0.0).astype(_BF)
    vec = lambda v: v.reshape(1, -1)

    h = x.reshape(t, D_MODEL)
    for li in range(depth):
        h = _ffn(h, vec(ffn1_norm[li]),
                 *_ffn_weights(ffn1_w_gate[li], ffn1_w_up[li], ffn1_w_down[li]))
        mw = _mixer_weights(w_in[li], q_lat_norm[li], w_uq[li], kv_lat_norm[li], w_ukv[li])
        sbq, sbk, sbvt, mq, mk, mv = _proj(h, vec(mix_norm[li]), mw, cos2, sin2, batch, seq)
        sb = _sb_attention(sbq, sbk, sbvt, tneg, batch, seq)
        mla = _mla_attention(mq, mk, mv, batch, seq)
        h = _out_proj(h, sb, mla, vec(sb_out_norm[li]), vec(mla_out_norm[li]),
                      w_out[li, :SB_WIDTH].astype(_BF), w_out[li, SB_WIDTH:].astype(_BF))
        h = _ffn(h, vec(ffn2_norm[li]),
                 *_ffn_weights(ffn2_w_gate[li], ffn2_w_up[li], ffn2_w_down[li]))
        h = _ple(h, p[li].reshape(t, D_PLE), vec(ple_norm[li]), w_ple_gate[li].astype(_BF),
                 w_ple_proj[li].astype(_BF), vec(final_norm), li == depth - 1)
    return h.reshape(batch, seq, D_MODEL)
```

```python
import functools
import math

import jax
import jax.numpy as jnp
from jax import lax
from jax.experimental import pallas as pl
from jax.experimental.pallas import tpu as pltpu

D_MODEL = 1024
D_PLE = 256
SB_HEADS = 8
SB_HEAD_DIM = 64
SB_WIDTH = SB_HEADS * SB_HEAD_DIM
MLA_HEADS = 4
MLA_NOPE_DIM = 128
MLA_ROPE_DIM = 64
MLA_QK_DIM = MLA_NOPE_DIM + MLA_ROPE_DIM
MLA_V_DIM = 128
MLA_WIDTH = MLA_HEADS * MLA_V_DIM
Q_LORA = 256
KV_LORA = 128
D_FF = 2816
ROPE_THETA = 10000.0
EPS = 1e-6

LANES = 128
MXU_DIM = 256
VMEM_LIMIT_BYTES = 56 * 1024 * 1024

FF_CHUNK = MXU_DIM
N_FF_CHUNKS = D_FF // FF_CHUNK
TOK_TILE = 512
ATT_TILE = 256
MLA_TILE = 512
MLA_SLAB = 2 * LANES
MLA_PREFETCH_HEADS = 2
SB_DEAD_LOG = -105.0

_BF = jnp.bfloat16
_F32 = jnp.float32
_NT = (((1,), (1,)), ((), ()))


def _rms(x, g):
    return x * lax.rsqrt(jnp.mean(x * x, axis=-1, keepdims=True) + EPS) * g


def _params(*sem):
    return pltpu.CompilerParams(dimension_semantics=sem, vmem_limit_bytes=VMEM_LIMIT_BYTES)


def _resident(shape):
    zeros = (0,) * len(shape)
    return pl.BlockSpec(shape, lambda *_: zeros, pipeline_mode=pl.Buffered(1))


def _ffn_kernel(x_ref, g_ref, wg_ref, wu_ref, wd_ref, o_ref):
    x = x_ref[...]
    xn = _rms(x, g_ref[...]).astype(_BF)
    acc = jnp.zeros(x.shape, _F32)
    for c in range(N_FF_CHUNKS):
        gate = jnp.dot(xn, wg_ref[c], preferred_element_type=_F32)
        up = jnp.dot(xn, wu_ref[c], preferred_element_type=_F32)
        hid = (gate * jax.nn.sigmoid(gate) * up).astype(_BF)
        acc = acc + jnp.dot(hid, wd_ref[c], preferred_element_type=_F32)
    o_ref[...] = x + 0.5 * acc


def _ffn(x, g, wg, wu, wd):
    t = x.shape[0]
    row = pl.BlockSpec((TOK_TILE, D_MODEL), lambda i: (i, 0))
    return pl.pallas_call(
        _ffn_kernel,
        grid=(t // TOK_TILE,),
        in_specs=[row, _resident(g.shape), _resident(wg.shape), _resident(wu.shape),
                  _resident(wd.shape)],
        out_specs=row,
        out_shape=jax.ShapeDtypeStruct(x.shape, _F32),
        compiler_params=_params("parallel"),
        name="ffn",
    )(x, g, wg, wu, wd)


def _proj_kernel(h_ref, g_ref, wq_ref, wk_ref, wvt_ref, wlat_ref, gq_ref, gkv_ref,
                 wuqa_ref, wuqb_ref, wuk_ref, wuvt_ref, cos_ref, sin_ref,
                 sbq_ref, sbk_ref, sbvt_ref, mq_ref, mk_ref, mvt_ref):
    u = _rms(h_ref[...], g_ref[...]).astype(_BF)
    sbq_ref[...] = jnp.dot(u, wq_ref[...], preferred_element_type=_F32).astype(_BF)
    sbk_ref[...] = jnp.dot(u, wk_ref[...], preferred_element_type=_F32).astype(_BF)
    sbvt_ref[0, 0] = lax.dot_general(wvt_ref[...], u, _NT,
                                     preferred_element_type=_F32).astype(_BF)

    lat = jnp.dot(u, wlat_ref[...], preferred_element_type=_F32)
    cos = cos_ref[...]
    sin = sin_ref[...]
    cq = _rms(lat[:, :Q_LORA], gq_ref[...]).astype(_BF)
    ckv = _rms(lat[:, Q_LORA:Q_LORA + KV_LORA], gkv_ref[...]).astype(_BF)
    o_kr = Q_LORA + KV_LORA
    k_rope = lat[:, o_kr:o_kr + LANES] * cos + lat[:, o_kr + LANES:o_kr + 2 * LANES] * sin
    k_rope = k_rope.astype(_BF)

    qa = jnp.dot(cq, wuqa_ref[...], preferred_element_type=_F32)
    qb = jnp.dot(cq, wuqb_ref[...], preferred_element_type=_F32)
    kn = jnp.dot(ckv, wuk_ref[...], preferred_element_type=_F32)
    mvt_ref[0, 0] = lax.dot_general(wuvt_ref[...], ckv, _NT,
                                    preferred_element_type=_F32).astype(_BF)
    scale = MLA_QK_DIM ** -0.5 * math.log2(math.e)
    for hd in range(MLA_HEADS):
        o = hd * MLA_SLAB
        mq_ref[:, o:o + LANES] = (qa[:, o:o + LANES] * scale).astype(_BF)
        roped = qa[:, o + LANES:o + 2 * LANES] * cos + qb[:, hd * LANES:(hd + 1) * LANES] * sin
        mq_ref[:, o + LANES:o + 2 * LANES] = (roped * scale).astype(_BF)
        mk_ref[:, o:o + LANES] = kn[:, hd * LANES:(hd + 1) * LANES].astype(_BF)
        mk_ref[:, o + LANES:o + 2 * LANES] = k_rope


def _proj(h, g, w, cos2, sin2, batch, seq):
    t = batch * seq
    tm = ATT_TILE
    nblk = seq // tm
    row = lambda n: pl.BlockSpec((tm, n), lambda i: (i, 0))
    outs = (
        jax.ShapeDtypeStruct((t, SB_WIDTH), _BF),
        jax.ShapeDtypeStruct((t, SB_WIDTH), _BF),
        jax.ShapeDtypeStruct((batch, nblk, SB_WIDTH, tm), _BF),
        jax.ShapeDtypeStruct((t, MLA_HEADS * MLA_SLAB), _BF),
        jax.ShapeDtypeStruct((t, MLA_HEADS * MLA_SLAB), _BF),
        jax.ShapeDtypeStruct((batch, nblk, MLA_WIDTH, tm), _BF),
    )
    weights = (g, w["wq"], w["wk"], w["wvt"], w["wlat"], w["gq"], w["gkv"],
               w["wuqa"], w["wuqb"], w["wuk"], w["wuvt"])
    vt_spec = pl.BlockSpec((1, 1, SB_WIDTH, tm), lambda i: (i // nblk, i % nblk, 0, 0))
    rope_spec = pl.BlockSpec((tm, LANES), lambda i: (i % nblk, 0))
    return pl.pallas_call(
        _proj_kernel,
        grid=(t // tm,),
        in_specs=[row(D_MODEL)] + [_resident(a.shape) for a in weights] + [rope_spec, rope_spec],
        out_specs=(row(SB_WIDTH), row(SB_WIDTH), vt_spec,
                   row(MLA_HEADS * MLA_SLAB), row(MLA_HEADS * MLA_SLAB), vt_spec),
        out_shape=outs,
        compiler_params=_params("parallel"),
        name="mixer_proj",
    )(h, *weights, cos2, sin2)


def _sb_tile(kb, vb, tneg, qh, carry, valid):
    z = lax.dot_general(kb, qh, _NT, preferred_element_type=_F32)
    sp = jnp.maximum(z, 0.0) + jnp.log(1.0 + jnp.exp(-jnp.abs(z)))
    if valid is not None:
        sp = jnp.where(valid, sp, 0.0)
    tail = jnp.dot(tneg, sp.astype(_BF), preferred_element_type=_F32)
    a = jnp.exp((z - sp) + tail + carry)
    if valid is not None:
        a = jnp.where(valid, a, 0.0)
    contrib = jnp.dot(vb, a.astype(_BF), preferred_element_type=_F32)
    return carry + tail[0:1, :] - sp[0:1, :], contrib


def _sb_kernel(q_ref, k_ref, vt_ref, tneg_ref, o_ref, carry_sc, acc_sc):
    tb = ATT_TILE
    hd = SB_HEAD_DIM
    i = pl.program_id(1)
    lane = lax.broadcasted_iota(jnp.int32, (tb, LANES), 1)
    tneg = tneg_ref[...]
    key_row = lax.broadcasted_iota(jnp.int32, (tb, tb), 0)
    qry_col = lax.broadcasted_iota(jnp.int32, (tb, tb), 1)
    valid = key_row < qry_col

    carry_sc[...] = jnp.zeros(carry_sc.shape, _F32)
    acc_sc[...] = jnp.zeros(acc_sc.shape, _F32)

    def all_heads(j, mask):
        rows = pl.ds(pl.multiple_of(j * tb, tb), tb)
        for pair in range(SB_HEADS // 2):
            cols = slice(pair * LANES, (pair + 1) * LANES)
            q = q_ref[:, cols]
            kb = k_ref[rows, cols]
            zero = jnp.zeros_like(q)
            for half in range(2):
                h = 2 * pair + half
                qh = jnp.where((lane >= hd) if half else (lane < hd), q, zero)
                vb = vt_ref[0, j, h * hd:(h + 1) * hd, :]
                carry, contrib = _sb_tile(kb, vb, tneg, qh, carry_sc[h], mask)
                carry_sc[h] = carry
                acc_sc[h * hd:(h + 1) * hd, :] += contrib

    def alive():
        return (jnp.max(carry_sc[...]) > SB_DEAD_LOG).astype(jnp.int32)

    all_heads(i, valid)

    def cond(state):
        j, live = state
        return jnp.logical_and(j >= 0, live > 0)

    def body(state):
        j, _ = state
        all_heads(j, None)
        return j - 1, alive()

    lax.while_loop(cond, body, (i - 1, alive()))
    o_ref[...] = acc_sc[...].T


def _sb_attention(sbq, sbk, sbvt, tneg, batch, seq):
    tb = ATT_TILE
    nblk = seq // tb
    return pl.pallas_call(
        _sb_kernel,
        grid=(batch, nblk),
        in_specs=[
            pl.BlockSpec((tb, SB_WIDTH), lambda b, i: (b * nblk + i, 0)),
            pl.BlockSpec((seq, SB_WIDTH), lambda b, i: (b, 0), pipeline_mode=pl.Buffered(1)),
            pl.BlockSpec((1, nblk, SB_WIDTH, tb), lambda b, i: (b, 0, 0, 0),
                         pipeline_mode=pl.Buffered(1)),
            _resident(tneg.shape),
        ],
        out_specs=pl.BlockSpec((tb, SB_WIDTH), lambda b, i: (b * nblk + i, 0)),
        out_shape=jax.ShapeDtypeStruct((batch * seq, SB_WIDTH), _F32),
        scratch_shapes=[pltpu.VMEM((SB_HEADS, 1, tb), _F32), pltpu.VMEM((SB_WIDTH, tb), _F32)],
        compiler_params=_params("parallel", "arbitrary"),
        name="sb_attention",
    )(sbq, sbk, sbvt, tneg)


def _mla_kernel(q_ref, k_ref, vt_ref, o_ref, m_sc, l_sc, acc_sc, s_sc):
    tb = MLA_TILE
    sub = ATT_TILE
    pre = MLA_PREFETCH_HEADS
    i = pl.program_id(1)
    m_sc[...] = jnp.full(m_sc.shape, -1e30, _F32)
    l_sc[...] = jnp.zeros(l_sc.shape, _F32)
    acc_sc[...] = jnp.zeros(acc_sc.shape, _F32)
    key_row = lax.broadcasted_iota(jnp.int32, (tb, tb), 0)
    qry_col = lax.broadcasted_iota(jnp.int32, (tb, tb), 1)
    causal = key_row <= qry_col

    def key_block(j, h):
        rows = pl.ds(pl.multiple_of(j * tb, tb), tb)
        return k_ref[rows, h * MLA_SLAB:(h + 1) * MLA_SLAB]

    def scores(kb, h):
        q = q_ref[:, h * MLA_SLAB:(h + 1) * MLA_SLAB]
        return lax.dot_general(kb, q, _NT, preferred_element_type=_F32)

    for h in range(pre):
        s_sc[h] = scores(key_block(0, h), h)

    def all_heads(j, valid, prefetch_next):
        k_now = {h: key_block(j, h) for h in range(pre, MLA_HEADS)}
        k_next = {h: key_block(j + 1, h) for h in range(pre)} if prefetch_next else {}
        vts = {(h, c): vt_ref[0, j * (tb // sub) + c, h * MLA_V_DIM:(h + 1) * MLA_V_DIM, :]
               for h in range(MLA_HEADS) for c in range(tb // sub)}
        s_of = {h: s_sc[h] for h in range(pre)}
        s_next = {h: scores(kb, h) for h, kb in k_next.items()}

        def softmax_step(h, s):
            if valid is not None:
                s = jnp.where(valid, s, -jnp.inf)
            m_old = m_sc[h]
            m_new = jnp.maximum(m_old, jnp.max(s, axis=0, keepdims=True))
            alpha = jnp.exp2(m_old - m_new)
            p = jnp.exp2(s - m_new)
            l_sc[h] = alpha * l_sc[h] + jnp.sum(p, axis=0, keepdims=True)
            m_sc[h] = m_new
            return alpha, p.astype(_BF)

        def accumulate(h, alpha, pb):
            pv = None
            for c in range(tb // sub):
                part = jnp.dot(vts[h, c], pb[c * sub:(c + 1) * sub],
                               preferred_element_type=_F32)
                pv = part if pv is None else pv + part
            acc_sc[h] = alpha * acc_sc[h] + pv

        for h in range(pre, MLA_HEADS):
            s_of[h] = scores(k_now[h], h)
        for h in range(MLA_HEADS):
            accumulate(h, *softmax_step(h, s_of.pop(h)))
        for h, s in s_next.items():
            s_sc[h] = s

    def body(j, c):
        all_heads(j, None, True)
        return c

    lax.fori_loop(0, i, body, 0)
    all_heads(i, causal, False)
    out_t = jnp.concatenate([acc_sc[h] / l_sc[h] for h in range(MLA_HEADS)], axis=0)
    o_ref[...] = out_t.T


def _mla_attention(mq, mk, mvt, batch, seq):
    tb = MLA_TILE
    nblk = seq // tb
    return pl.pallas_call(
        _mla_kernel,
        grid=(batch, nblk),
        in_specs=[
            pl.BlockSpec((tb, MLA_HEADS * MLA_SLAB), lambda b, i: (b * nblk + i, 0)),
            pl.BlockSpec((seq, MLA_HEADS * MLA_SLAB), lambda b, i: (b, 0),
                         pipeline_mode=pl.Buffered(1)),
            pl.BlockSpec((1,) + mvt.shape[1:], lambda b, i: (b, 0, 0, 0),
                         pipeline_mode=pl.Buffered(1)),
        ],
        out_specs=pl.BlockSpec((tb, MLA_WIDTH), lambda b, i: (b * nblk + i, 0)),
        out_shape=jax.ShapeDtypeStruct((batch * seq, MLA_WIDTH), _F32),
        scratch_shapes=[pltpu.VMEM((MLA_HEADS, 1, tb), _F32), pltpu.VMEM((MLA_HEADS, 1, tb), _F32),
                        pltpu.VMEM((MLA_HEADS, MLA_V_DIM, tb), _F32),
                        pltpu.VMEM((MLA_PREFETCH_HEADS, tb, tb), _F32)],
        compiler_params=_params("parallel", "arbitrary"),
        name="mla_attention",
    )(mq, mk, mvt)


def _out_kernel(h_ref, sb_ref, mla_ref, gsb_ref, gmla_ref, wsb_ref, wmla_ref, o_ref):
    sb = _rms(sb_ref[...], gsb_ref[...]).astype(_BF)
    ml = _rms(mla_ref[...], gmla_ref[...]).astype(_BF)
    y = jnp.dot(sb, wsb_ref[...], preferred_element_type=_F32)
    y = y + jnp.dot(ml, wmla_ref[...], preferred_element_type=_F32)
    o_ref[...] = h_ref[...] + y


def _out_proj(h, sb, mla, gsb, gmla, wsb, wmla):
    t = h.shape[0]
    row = lambda n: pl.BlockSpec((TOK_TILE, n), lambda i: (i, 0))
    return pl.pallas_call(
        _out_kernel,
        grid=(t // TOK_TILE,),
        in_specs=[row(D_MODEL), row(SB_WIDTH), row(MLA_WIDTH), _resident(gsb.shape),
                  _resident(gmla.shape), _resident(wsb.shape), _resident(wmla.shape)],
        out_specs=row(D_MODEL),
        out_shape=jax.ShapeDtypeStruct(h.shape, _F32),
        compiler_params=_params("parallel"),
        name="out_proj",
    )(h, sb, mla, gsb, gmla, wsb, wmla)


def _ple_kernel(h_ref, p_ref, g_ref, wg_ref, wp_ref, gf_ref, o_ref, *, final_norm):
    h = h_ref[...]
    hn = _rms(h, g_ref[...]).astype(_BF)
    gate = jax.nn.sigmoid(jnp.dot(hn, wg_ref[...], preferred_element_type=_F32))
    emb = jnp.dot(p_ref[...].astype(_BF), wp_ref[...], preferred_element_type=_F32)
    out = h + gate * emb
    if final_norm:
        out = _rms(out, gf_ref[...])
    o_ref[...] = out


def _ple(h, p, g, wg, wp, gf, final_norm):
    t = h.shape[0]
    row = lambda n: pl.BlockSpec((TOK_TILE, n), lambda i: (i, 0))
    return pl.pallas_call(
        functools.partial(_ple_kernel, final_norm=final_norm),
        grid=(t // TOK_TILE,),
        in_specs=[row(D_MODEL), row(D_PLE), _resident(g.shape), _resident(wg.shape),
                  _resident(wp.shape), _resident(gf.shape)],
        out_specs=row(D_MODEL),
        out_shape=jax.ShapeDtypeStruct(h.shape, _F32),
        compiler_params=_params("parallel"),
        name="ple",
    )(h, p, g, wg, wp, gf)


def _swap_cols(w):
    half = w.shape[-1] // 2
    return jnp.concatenate([-w[..., half:], w[..., :half]], axis=-1)


def _pad_cols(w, n):
    return jnp.pad(w, ((0, 0), (0, n - w.shape[-1])))


def _ffn_weights(w_gate, w_up, w_down):
    split = lambda w: w.reshape(D_MODEL, N_FF_CHUNKS, FF_CHUNK).transpose(1, 0, 2).astype(_BF)
    return split(w_gate), split(w_up), w_down.reshape(N_FF_CHUNKS, FF_CHUNK, D_MODEL).astype(_BF)


def _mixer_weights(w_in, q_lat_norm, w_uq, kv_lat_norm, w_ukv):
    o1, o2, o3 = SB_WIDTH, 2 * SB_WIDTH, 3 * SB_WIDTH
    o4 = o3 + Q_LORA
    o5 = o4 + KV_LORA
    w_kr = w_in[:, o5:]
    wlat = jnp.concatenate([w_in[:, o3:o5], _pad_cols(w_kr, LANES),
                            _pad_cols(_swap_cols(w_kr), LANES)], axis=1)
    uq = w_uq.reshape(Q_LORA, MLA_HEADS, MLA_QK_DIM)
    rope_w = uq[:, :, MLA_NOPE_DIM:]
    wuqa = jnp.pad(uq, ((0, 0), (0, 0), (0, MLA_SLAB - MLA_QK_DIM)))
    wuqb = jnp.pad(_swap_cols(rope_w), ((0, 0), (0, 0), (0, LANES - MLA_ROPE_DIM)))
    ukv = w_ukv.reshape(KV_LORA, MLA_HEADS, MLA_NOPE_DIM + MLA_V_DIM)
    wuk = ukv[:, :, :MLA_NOPE_DIM].reshape(KV_LORA, -1)
    wuv = ukv[:, :, MLA_NOPE_DIM:].reshape(KV_LORA, -1)
    return {
        "wq": (w_in[:, :o1] * SB_HEAD_DIM ** -0.5).astype(_BF),
        "wk": w_in[:, o1:o2].astype(_BF),
        "wvt": w_in[:, o2:o3].T.astype(_BF),
        "wlat": wlat.astype(_BF),
        "gq": q_lat_norm.reshape(1, -1),
        "gkv": kv_lat_norm.reshape(1, -1),
        "wuqa": wuqa.reshape(Q_LORA, -1).astype(_BF),
        "wuqb": wuqb.reshape(Q_LORA, -1).astype(_BF),
        "wuk": wuk.astype(_BF),
        "wuvt": wuv.T.astype(_BF),
    }


def _rope_tables(seq):
    half = MLA_ROPE_DIM // 2
    inv_freq = ROPE_THETA ** (-jnp.arange(half, dtype=_F32) / half)
    ang = jnp.arange(seq, dtype=_F32)[:, None] * inv_freq[None, :]
    cos, sin = jnp.cos(ang), jnp.sin(ang)
    pad = jnp.zeros((seq, LANES - MLA_ROPE_DIM), _F32)
    return (jnp.concatenate([cos, cos, pad], axis=1), jnp.concatenate([sin, sin, pad], axis=1))


def kernel(x, p, ffn1_norm, ffn1_w_gate, ffn1_w_up, ffn1_w_down, mix_norm, w_in, q_lat_norm, w_uq, kv_lat_norm, w_ukv, sb_out_norm, mla_out_norm, w_out, ffn2_norm, ffn2_w_gate, ffn2_w_up, ffn2_w_down, ple_norm, w_ple_gate, w_ple_proj, final_norm):
    batch, seq, _ = x.shape
    depth = p.shape[0]
    assert seq % MLA_TILE == 0 and seq % ATT_TILE == 0 and (batch * seq) % TOK_TILE == 0
    t = batch * seq
    cos2, sin2 = _rope_tables(seq)
    idx = jnp.arange(ATT_TILE)
    tneg = jnp.where(idx[None, :] > idx[:, None], -1.0, 0.0).astype(_BF)
    vec = lambda v: v.reshape(1, -1)

    h = x.reshape(t, D_MODEL)
    for li in range(depth):
        h = _ffn(h, vec(ffn1_norm[li]),
                 *_ffn_weights(ffn1_w_gate[li], ffn1_w_up[li], ffn1_w_down[li]))
        mw = _mixer_weights(w_in[li], q_lat_norm[li], w_uq[li], kv_lat_norm[li], w_ukv[li])
        sbq, sbk, sbvt, mq, mk, mvt = _proj(h, vec(mix_norm[li]), mw, cos2, sin2, batch, seq)
        sb = _sb_attention(sbq, sbk, sbvt, tneg, batch, seq)
        mla = _mla_attention(mq, mk, mvt, batch, seq)
        h = _out_proj(h, sb, mla, vec(sb_out_norm[li]), vec(mla_out_norm[li]),
                      w_out[li, :SB_WIDTH].astype(_BF), w_out[li, SB_WIDTH:].astype(_BF))
        h = _ffn(h, vec(ffn2_norm[li]),
                 *_ffn_weights(ffn2_w_gate[li], ffn2_w_up[li], ffn2_w_down[li]))
        h = _ple(h, p[li].reshape(t, D_PLE), vec(ple_norm[li]), w_ple_gate[li].astype(_BF),
                 w_ple_proj[li].astype(_BF), vec(final_norm), li == depth - 1)
    return h.reshape(batch, seq, D_MODEL)
```

```python
import functools
import math

import jax
import jax.numpy as jnp
from jax import lax
from jax.experimental import pallas as pl
from jax.experimental.pallas import tpu as pltpu

D_MODEL = 1024
D_PLE = 256
SB_HEADS = 8
SB_HEAD_DIM = 64
SB_WIDTH = SB_HEADS * SB_HEAD_DIM
MLA_HEADS = 4
MLA_NOPE_DIM = 128
MLA_ROPE_DIM = 64
MLA_QK_DIM = MLA_NOPE_DIM + MLA_ROPE_DIM
MLA_V_DIM = 128
MLA_WIDTH = MLA_HEADS * MLA_V_DIM
Q_LORA = 256
KV_LORA = 128
D_FF = 2816
ROPE_THETA = 10000.0
EPS = 1e-6

LANES = 128
MXU_DIM = 256
VMEM_LIMIT_BYTES = 56 * 1024 * 1024

FF_CHUNK = MXU_DIM
N_FF_CHUNKS = D_FF // FF_CHUNK
TOK_TILE = 512
ATT_TILE = 256
MLA_TILE = 512
MLA_SLAB = 2 * LANES
MLA_PREFETCH_HEADS = 2
SB_DEAD_LOG2 = -152.0

_BF = jnp.bfloat16
_F32 = jnp.float32
_NT = (((1,), (1,)), ((), ()))


def _rms(x, g):
    return x * lax.rsqrt(jnp.mean(x * x, axis=-1, keepdims=True) + EPS) * g


def _params(*sem):
    return pltpu.CompilerParams(dimension_semantics=sem, vmem_limit_bytes=VMEM_LIMIT_BYTES)


def _resident(shape):
    zeros = (0,) * len(shape)
    return pl.BlockSpec(shape, lambda *_: zeros, pipeline_mode=pl.Buffered(1))


def _ffn_kernel(x_ref, g_ref, wg_ref, wu_ref, wd_ref, o_ref):
    x = x_ref[...]
    xn = _rms(x, g_ref[...]).astype(_BF)
    acc = jnp.zeros(x.shape, _F32)
    for c in range(N_FF_CHUNKS):
        cols = slice(c * FF_CHUNK, (c + 1) * FF_CHUNK)
        gate = jnp.dot(xn, wg_ref[:, cols], preferred_element_type=_F32)
        up = jnp.dot(xn, wu_ref[:, cols], preferred_element_type=_F32)
        hid = (gate * jax.nn.sigmoid(gate) * up).astype(_BF)
        acc = acc + jnp.dot(hid, wd_ref[cols, :], preferred_element_type=_F32)
    o_ref[...] = x + 0.5 * acc


def _ffn(x, g, wg, wu, wd):
    t = x.shape[0]
    row = pl.BlockSpec((TOK_TILE, D_MODEL), lambda i: (i, 0))
    return pl.pallas_call(
        _ffn_kernel,
        grid=(t // TOK_TILE,),
        in_specs=[row, _resident(g.shape), _resident(wg.shape), _resident(wu.shape),
                  _resident(wd.shape)],
        out_specs=row,
        out_shape=jax.ShapeDtypeStruct(x.shape, _F32),
        compiler_params=_params("parallel"),
        name="ffn",
    )(x, g, wg, wu, wd)


def _proj_kernel(h_ref, g_ref, wq_ref, wk_ref, wvt_ref, wlat_ref, gq_ref, gkv_ref,
                 wuqa_ref, wuqb_ref, wuk_ref, wuvt_ref, cos_ref, sin_ref,
                 sbq_ref, sbk_ref, sbvt_ref, mq_ref, mk_ref, mvt_ref):
    u = _rms(h_ref[...], g_ref[...]).astype(_BF)
    sb_scale = SB_HEAD_DIM ** -0.5 * math.log2(math.e)
    sbq_ref[...] = (jnp.dot(u, wq_ref[...], preferred_element_type=_F32) * sb_scale).astype(_BF)
    sbk_ref[...] = jnp.dot(u, wk_ref[...], preferred_element_type=_F32).astype(_BF)
    sbvt_ref[0, 0] = lax.dot_general(wvt_ref[...], u, _NT,
                                     preferred_element_type=_F32).astype(_BF)

    lat = jnp.dot(u, wlat_ref[...], preferred_element_type=_F32)
    cos = cos_ref[...]
    sin = sin_ref[...]
    cq = _rms(lat[:, :Q_LORA], gq_ref[...]).astype(_BF)
    ckv = _rms(lat[:, Q_LORA:Q_LORA + KV_LORA], gkv_ref[...]).astype(_BF)
    o_kr = Q_LORA + KV_LORA
    k_rope = lat[:, o_kr:o_kr + LANES] * cos + lat[:, o_kr + LANES:o_kr + 2 * LANES] * sin
    k_rope = k_rope.astype(_BF)

    qa = jnp.dot(cq, wuqa_ref[...], preferred_element_type=_F32)
    qb = jnp.dot(cq, wuqb_ref[...], preferred_element_type=_F32)
    kn = jnp.dot(ckv, wuk_ref[...], preferred_element_type=_F32)
    mvt_ref[0, 0] = lax.dot_general(wuvt_ref[...], ckv, _NT,
                                    preferred_element_type=_F32).astype(_BF)
    scale = MLA_QK_DIM ** -0.5 * math.log2(math.e)
    for hd in range(MLA_HEADS):
        o = hd * MLA_SLAB
        mq_ref[:, o:o + LANES] = (qa[:, o:o + LANES] * scale).astype(_BF)
        roped = qa[:, o + LANES:o + 2 * LANES] * cos + qb[:, hd * LANES:(hd + 1) * LANES] * sin
        mq_ref[:, o + LANES:o + 2 * LANES] = (roped * scale).astype(_BF)
        mk_ref[:, o:o + LANES] = kn[:, hd * LANES:(hd + 1) * LANES].astype(_BF)
        mk_ref[:, o + LANES:o + 2 * LANES] = k_rope


def _proj(h, g, w, cos2, sin2, batch, seq):
    t = batch * seq
    tm = ATT_TILE
    nblk = seq // tm
    row = lambda n: pl.BlockSpec((tm, n), lambda i: (i, 0))
    outs = (
        jax.ShapeDtypeStruct((t, SB_WIDTH), _BF),
        jax.ShapeDtypeStruct((t, SB_WIDTH), _BF),
        jax.ShapeDtypeStruct((batch, nblk, SB_WIDTH, tm), _BF),
        jax.ShapeDtypeStruct((t, MLA_HEADS * MLA_SLAB), _BF),
        jax.ShapeDtypeStruct((t, MLA_HEADS * MLA_SLAB), _BF),
        jax.ShapeDtypeStruct((batch, nblk, MLA_WIDTH, tm), _BF),
    )
    weights = (g, w["wq"], w["wk"], w["wvt"], w["wlat"], w["gq"], w["gkv"],
               w["wuqa"], w["wuqb"], w["wuk"], w["wuvt"])
    vt_spec = pl.BlockSpec((1, 1, SB_WIDTH, tm), lambda i: (i // nblk, i % nblk, 0, 0))
    rope_spec = pl.BlockSpec((tm, LANES), lambda i: (i % nblk, 0))
    return pl.pallas_call(
        _proj_kernel,
        grid=(t // tm,),
        in_specs=[row(D_MODEL)] + [_resident(a.shape) for a in weights] + [rope_spec, rope_spec],
        out_specs=(row(SB_WIDTH), row(SB_WIDTH), vt_spec,
                   row(MLA_HEADS * MLA_SLAB), row(MLA_HEADS * MLA_SLAB), vt_spec),
        out_shape=outs,
        compiler_params=_params("parallel"),
        name="mixer_proj",
    )(h, *weights, cos2, sin2)


def _sb_kernel(q_ref, k_ref, vt_ref, tneg_ref, o_ref, carry_sc, acc_sc):
    tb = ATT_TILE
    hd = SB_HEAD_DIM
    i = pl.program_id(1)
    lane = lax.broadcasted_iota(jnp.int32, (tb, LANES), 1)
    tneg = tneg_ref[...]
    key_row = lax.broadcasted_iota(jnp.int32, (tb, SB_HEADS * tb), 0)
    qry_col = lax.broadcasted_iota(jnp.int32, (tb, SB_HEADS * tb), 1) & (tb - 1)
    valid = key_row < qry_col

    carry_sc[...] = jnp.zeros(carry_sc.shape, _F32)
    acc_sc[...] = jnp.zeros(acc_sc.shape, _F32)

    q_pairs = []
    for pair in range(SB_HEADS // 2):
        q = q_ref[:, pair * LANES:(pair + 1) * LANES]
        zero = jnp.zeros_like(q)
        q_pairs.append(jnp.concatenate([jnp.where(lane < hd, q, zero),
                                        jnp.where(lane >= hd, q, zero)], axis=0))

    def all_heads(j, mask):
        rows = pl.ds(pl.multiple_of(j * tb, tb), tb)
        kbs = [k_ref[rows, pair * LANES:(pair + 1) * LANES] for pair in range(SB_HEADS // 2)]
        vbs = [vt_ref[0, j, h * hd:(h + 1) * hd, :] for h in range(SB_HEADS)]
        carry = carry_sc[...]
        z = jnp.concatenate([lax.dot_general(kb, qp, _NT, preferred_element_type=_F32)
                             for kb, qp in zip(kbs, q_pairs)], axis=1)
        sp = jnp.maximum(z, 0.0) + jnp.log2(1.0 + jnp.exp2(-jnp.abs(z)))
        if mask is not None:
            sp = jnp.where(mask, sp, 0.0)
        tail = jnp.dot(tneg, sp.astype(_BF), preferred_element_type=_F32)
        a = jnp.exp2((z - sp) + tail + carry)
        if mask is not None:
            a = jnp.where(mask, a, 0.0)
        ab = a.astype(_BF)
        contrib = jnp.concatenate(
            [jnp.dot(vbs[h], ab[:, h * tb:(h + 1) * tb], preferred_element_type=_F32)
             for h in range(SB_HEADS)], axis=0)
        carry_sc[...] = carry + tail[0:1, :] - sp[0:1, :]
        acc_sc[...] += contrib

    def alive():
        return (jnp.max(carry_sc[...]) > SB_DEAD_LOG2).astype(jnp.int32)

    all_heads(i, valid)

    def cond(state):
        j, live = state
        return jnp.logical_and(j >= 0, live > 0)

    def body(state):
        j, _ = state
        all_heads(j, None)
        return j - 1, alive()

    lax.while_loop(cond, body, (i - 1, alive()))
    o_ref[...] = acc_sc[...].T


def _sb_attention(sbq, sbk, sbvt, tneg, batch, seq):
    tb = ATT_TILE
    nblk = seq // tb
    return pl.pallas_call(
        _sb_kernel,
        grid=(batch, nblk),
        in_specs=[
            pl.BlockSpec((tb, SB_WIDTH), lambda b, i: (b * nblk + i, 0)),
            pl.BlockSpec((seq, SB_WIDTH), lambda b, i: (b, 0), pipeline_mode=pl.Buffered(1)),
            pl.BlockSpec((1, nblk, SB_WIDTH, tb), lambda b, i: (b, 0, 0, 0),
                         pipeline_mode=pl.Buffered(1)),
            _resident(tneg.shape),
        ],
        out_specs=pl.BlockSpec((tb, SB_WIDTH), lambda b, i: (b * nblk + i, 0)),
        out_shape=jax.ShapeDtypeStruct((batch * seq, SB_WIDTH), _F32),
        scratch_shapes=[pltpu.VMEM((1, SB_HEADS * tb), _F32), pltpu.VMEM((SB_WIDTH, tb), _F32)],
        compiler_params=_params("parallel", "arbitrary"),
        name="sb_attention",
    )(sbq, sbk, sbvt, tneg)


def _mla_kernel(q_ref, k_ref, vt_ref, o_ref, m_sc, l_sc, acc_sc, s_sc):
    tb = MLA_TILE
    sub = ATT_TILE
    pre = MLA_PREFETCH_HEADS
    i = pl.program_id(1)
    m_sc[...] = jnp.full(m_sc.shape, -1e30, _F32)
    l_sc[...] = jnp.zeros(l_sc.shape, _F32)
    acc_sc[...] = jnp.zeros(acc_sc.shape, _F32)
    key_row = lax.broadcasted_iota(jnp.int32, (tb, tb), 0)
    qry_col = lax.broadcasted_iota(jnp.int32, (tb, tb), 1)
    causal = key_row <= qry_col

    def key_block(j, h):
        rows = pl.ds(pl.multiple_of(j * tb, tb), tb)
        return k_ref[rows, h * MLA_SLAB:(h + 1) * MLA_SLAB]

    def scores(kb, h):
        q = q_ref[:, h * MLA_SLAB:(h + 1) * MLA_SLAB]
        return lax.dot_general(kb, q, _NT, preferred_element_type=_F32)

    for h in range(pre):
        s_sc[h] = scores(key_block(0, h), h)

    def all_heads(j, valid, prefetch_next):
        k_now = {h: key_block(j, h) for h in range(pre, MLA_HEADS)}
        k_next = {h: key_block(j + 1, h) for h in range(pre)} if prefetch_next else {}
        vts = {(h, c): vt_ref[0, j * (tb // sub) + c, h * MLA_V_DIM:(h + 1) * MLA_V_DIM, :]
               for h in range(MLA_HEADS) for c in range(tb // sub)}
        s_of = {h: s_sc[h] for h in range(pre)}
        s_next = {h: scores(kb, h) for h, kb in k_next.items()}

        def softmax_step(h, s):
            if valid is not None:
                s = jnp.where(valid, s, -jnp.inf)
            m_old = m_sc[h]
            m_new = jnp.maximum(m_old, jnp.max(s, axis=0, keepdims=True))
            alpha = jnp.exp2(m_old - m_new)
            p = jnp.exp2(s - m_new)
            l_sc[h] = alpha * l_sc[h] + jnp.sum(p, axis=0, keepdims=True)
            m_sc[h] = m_new
            return alpha, p.astype(_BF)

        def accumulate(h, alpha, pb):
            pv = None
            for c in range(tb // sub):
                part = jnp.dot(vts[h, c], pb[c * sub:(c + 1) * sub],
                               preferred_element_type=_F32)
                pv = part if pv is None else pv + part
            acc_sc[h] = alpha * acc_sc[h] + pv

        for h in range(pre, MLA_HEADS):
            s_of[h] = scores(k_now[h], h)
        for h in range(MLA_HEADS):
            accumulate(h, *softmax_step(h, s_of.pop(h)))
        for h, s in s_next.items():
            s_sc[h] = s

    def body(j, c):
        all_heads(j, None, True)
        return c

    lax.fori_loop(0, i, body, 0)
    all_heads(i, causal, False)
    out_t = jnp.concatenate([acc_sc[h] / l_sc[h] for h in range(MLA_HEADS)], axis=0)
    o_ref[...] = out_t.T


def _mla_attention(mq, mk, mvt, batch, seq):
    tb = MLA_TILE
    nblk = seq // tb
    return pl.pallas_call(
        _mla_kernel,
        grid=(batch, nblk),
        in_specs=[
            pl.BlockSpec((tb, MLA_HEADS * MLA_SLAB), lambda b, i: (b * nblk + i, 0)),
            pl.BlockSpec((seq, MLA_HEADS * MLA_SLAB), lambda b, i: (b, 0),
                         pipeline_mode=pl.Buffered(1)),
            pl.BlockSpec((1,) + mvt.shape[1:], lambda b, i: (b, 0, 0, 0),
                         pipeline_mode=pl.Buffered(1)),
        ],
        out_specs=pl.BlockSpec((tb, MLA_WIDTH), lambda b, i: (b * nblk + i, 0)),
        out_shape=jax.ShapeDtypeStruct((batch * seq, MLA_WIDTH), _F32),
        scratch_shapes=[pltpu.VMEM((MLA_HEADS, 1, tb), _F32), pltpu.VMEM((MLA_HEADS, 1, tb), _F32),
                        pltpu.VMEM((MLA_HEADS, MLA_V_DIM, tb), _F32),
                        pltpu.VMEM((MLA_PREFETCH_HEADS, tb, tb), _F32)],
        compiler_params=_params("parallel", "arbitrary"),
        name="mla_attention",
    )(mq, mk, mvt)


def _out_kernel(h_ref, sb_ref, mla_ref, gsb_ref, gmla_ref, wsb_ref, wmla_ref, o_ref):
    sb = _rms(sb_ref[...], gsb_ref[...]).astype(_BF)
    ml = _rms(mla_ref[...], gmla_ref[...]).astype(_BF)
    y = jnp.dot(sb, wsb_ref[...], preferred_element_type=_F32)
    y = y + jnp.dot(ml, wmla_ref[...], preferred_element_type=_F32)
    o_ref[...] = h_ref[...] + y


def _out_proj(h, sb, mla, gsb, gmla, wsb, wmla):
    t = h.shape[0]
    row = lambda n: pl.BlockSpec((TOK_TILE, n), lambda i: (i, 0))
    return pl.pallas_call(
        _out_kernel,
        grid=(t // TOK_TILE,),
        in_specs=[row(D_MODEL), row(SB_WIDTH), row(MLA_WIDTH), _resident(gsb.shape),
                  _resident(gmla.shape), _resident(wsb.shape), _resident(wmla.shape)],
        out_specs=row(D_MODEL),
        out_shape=jax.ShapeDtypeStruct(h.shape, _F32),
        compiler_params=_params("parallel"),
        name="out_proj",
    )(h, sb, mla, gsb, gmla, wsb, wmla)


def _ple_kernel(h_ref, p_ref, g_ref, wg_ref, wp_ref, gf_ref, o_ref, *, final_norm):
    h = h_ref[...]
    hn = _rms(h, g_ref[...]).astype(_BF)
    gate = jax.nn.sigmoid(jnp.dot(hn, wg_ref[...], preferred_element_type=_F32))
    emb = jnp.dot(p_ref[...].astype(_BF), wp_ref[...], preferred_element_type=_F32)
    out = h + gate * emb
    if final_norm:
        out = _rms(out, gf_ref[...])
    o_ref[...] = out


def _ple(h, p, g, wg, wp, gf, final_norm):
    t = h.shape[0]
    row = lambda n: pl.BlockSpec((TOK_TILE, n), lambda i: (i, 0))
    return pl.pallas_call(
        functools.partial(_ple_kernel, final_norm=final_norm),
        grid=(t // TOK_TILE,),
        in_specs=[row(D_MODEL), row(D_PLE), _resident(g.shape), _resident(wg.shape),
                  _resident(wp.shape), _resident(gf.shape)],
        out_specs=row(D_MODEL),
        out_shape=jax.ShapeDtypeStruct(h.shape, _F32),
        compiler_params=_params("parallel"),
        name="ple",
    )(h, p, g, wg, wp, gf)


def _swap_cols(w):
    half = w.shape[-1] // 2
    return jnp.concatenate([-w[..., half:], w[..., :half]], axis=-1)


def _pad_cols(w, n):
    return jnp.pad(w, ((0, 0), (0, n - w.shape[-1])))


def _ffn_weights(w_gate, w_up, w_down):
    return w_gate.astype(_BF), w_up.astype(_BF), w_down.astype(_BF)


def _mixer_weights(w_in, q_lat_norm, w_uq, kv_lat_norm, w_ukv):
    o1, o2, o3 = SB_WIDTH, 2 * SB_WIDTH, 3 * SB_WIDTH
    o4 = o3 + Q_LORA
    o5 = o4 + KV_LORA
    w_kr = w_in[:, o5:]
    wlat = jnp.concatenate([w_in[:, o3:o5], _pad_cols(w_kr, LANES),
                            _pad_cols(_swap_cols(w_kr), LANES)], axis=1)
    uq = w_uq.reshape(Q_LORA, MLA_HEADS, MLA_QK_DIM)
    rope_w = uq[:, :, MLA_NOPE_DIM:]
    wuqa = jnp.pad(uq, ((0, 0), (0, 0), (0, MLA_SLAB - MLA_QK_DIM)))
    wuqb = jnp.pad(_swap_cols(rope_w), ((0, 0), (0, 0), (0, LANES - MLA_ROPE_DIM)))
    ukv = w_ukv.reshape(KV_LORA, MLA_HEADS, MLA_NOPE_DIM + MLA_V_DIM)
    wuk = ukv[:, :, :MLA_NOPE_DIM].reshape(KV_LORA, -1)
    wuv = ukv[:, :, MLA_NOPE_DIM:].reshape(KV_LORA, -1)
    return {
        "wq": w_in[:, :o1].astype(_BF),
        "wk": w_in[:, o1:o2].astype(_BF),
        "wvt": w_in[:, o2:o3].T.astype(_BF),
        "wlat": wlat.astype(_BF),
        "gq": q_lat_norm.reshape(1, -1),
        "gkv": kv_lat_norm.reshape(1, -1),
        "wuqa": wuqa.reshape(Q_LORA, -1).astype(_BF),
        "wuqb": wuqb.reshape(Q_LORA, -1).astype(_BF),
        "wuk": wuk.astype(_BF),
        "wuvt": wuv.T.astype(_BF),
    }


def _rope_tables(seq):
    half = MLA_ROPE_DIM // 2
    inv_freq = ROPE_THETA ** (-jnp.arange(half, dtype=_F32) / half)
    ang = jnp.arange(seq, dtype=_F32)[:, None] * inv_freq[None, :]
    cos, sin = jnp.cos(ang), jnp.sin(ang)
    pad = jnp.zeros((seq, LANES - MLA_ROPE_DIM), _F32)
    return (jnp.concatenate([cos, cos, pad], axis=1), jnp.concatenate([sin, sin, pad], axis=1))


def kernel(x, p, ffn1_norm, ffn1_w_gate, ffn1_w_up, ffn1_w_down, mix_norm, w_in, q_lat_norm, w_uq, kv_lat_norm, w_ukv, sb_out_norm, mla_out_norm, w_out, ffn2_norm, ffn2_w_gate, ffn2_w_up, ffn2_w_down, ple_norm, w_ple_gate, w_ple_proj, final_norm):
    batch, seq, _ = x.shape
    depth = p.shape[0]
    assert seq % MLA_TILE == 0 and seq % ATT_TILE == 0 and (batch * seq) % TOK_TILE == 0
    t = batch * seq
    cos2, sin2 = _rope_tables(seq)
    idx = jnp.arange(ATT_TILE)
    tneg = jnp.where(idx[None, :] > idx[:, None], -1.0, 0.0).astype(_BF)
    vec = lambda v: v.reshape(1, -1)

    h = x.reshape(t, D_MODEL)
    for li in range(depth):
        h = _ffn(h, vec(ffn1_norm[li]),
                 *_ffn_weights(ffn1_w_gate[li], ffn1_w_up[li], ffn1_w_down[li]))
        mw = _mixer_weights(w_in[li], q_lat_norm[li], w_uq[li], kv_lat_norm[li], w_ukv[li])
        sbq, sbk, sbvt, mq, mk, mvt = _proj(h, vec(mix_norm[li]), mw, cos2, sin2, batch, seq)
        sb = _sb_attention(sbq, sbk, sbvt, tneg, batch, seq)
        mla = _mla_attention(mq, mk, mvt, batch, seq)
        h = _out_proj(h, sb, mla, vec(sb_out_norm[li]), vec(mla_out_norm[li]),
                      w_out[li, :SB_WIDTH].astype(_BF), w_out[li, SB_WIDTH:].astype(_BF))
        h = _ffn(h, vec(ffn2_norm[li]),
                 *_ffn_weights(ffn2_w_gate[li], ffn2_w_up[li], ffn2_w_down[li]))
        h = _ple(h, p[li].reshape(t, D_PLE), vec(ple_norm[li]), w_ple_gate[li].astype(_BF),
                 w_ple_proj[li].astype(_BF), vec(final_norm), li == depth - 1)
    return h.reshape(batch, seq, D_MODEL)
```

```python
import functools
import math

import jax
import jax.numpy as jnp
from jax import lax
from jax.experimental import pallas as pl
from jax.experimental.pallas import tpu as pltpu

D_MODEL = 1024
D_PLE = 256
SB_HEADS = 8
SB_HEAD_DIM = 64
SB_WIDTH = SB_HEADS * SB_HEAD_DIM
MLA_HEADS = 4
MLA_NOPE_DIM = 128
MLA_ROPE_DIM = 64
MLA_QK_DIM = MLA_NOPE_DIM + MLA_ROPE_DIM
MLA_V_DIM = 128
MLA_WIDTH = MLA_HEADS * MLA_V_DIM
Q_LORA = 256
KV_LORA = 128
D_FF = 2816
ROPE_THETA = 10000.0
EPS = 1e-6

LANES = 128
MXU_DIM = 256
VMEM_LIMIT_BYTES = 58 * 1024 * 1024

FF_CHUNK = MXU_DIM
N_FF_CHUNKS = D_FF // FF_CHUNK
TOK_TILE = 512
POST_TILE = 512
ATT_TILE = 256
MLA_TILE = 512
MLA_SLAB = 2 * LANES
MLA_PREFETCH_HEADS = 2
SB_DEAD_LOG2 = -152.0

_BF = jnp.bfloat16
_F32 = jnp.float32
_NT = (((1,), (1,)), ((), ()))


def _rms(x, g):
    return x * lax.rsqrt(jnp.mean(x * x, axis=-1, keepdims=True) + EPS) * g


def _params(*sem):
    return pltpu.CompilerParams(dimension_semantics=sem, vmem_limit_bytes=VMEM_LIMIT_BYTES)


def _resident(shape):
    zeros = (0,) * len(shape)
    return pl.BlockSpec(shape, lambda *_: zeros, pipeline_mode=pl.Buffered(1))


def _layer(arr, li):
    index = (li,) + (0,) * (arr.ndim - 1)
    return pl.BlockSpec((None,) + arr.shape[1:], lambda *_: index, pipeline_mode=pl.Buffered(1))


def _swiglu_half_step(x, g_ref, wg_ref, wu_ref, wd_ref):
    xn = _rms(x, g_ref[...]).astype(_BF)
    acc = jnp.zeros(x.shape, _F32)
    for c in range(N_FF_CHUNKS):
        cols = slice(c * FF_CHUNK, (c + 1) * FF_CHUNK)
        gate = jnp.dot(xn, wg_ref[:, cols].astype(_BF), preferred_element_type=_F32)
        up = jnp.dot(xn, wu_ref[:, cols].astype(_BF), preferred_element_type=_F32)
        hid = (gate * jax.nn.sigmoid(gate) * up).astype(_BF)
        acc = acc + jnp.dot(hid, wd_ref[cols, :].astype(_BF), preferred_element_type=_F32)
    return x + 0.5 * acc


def _ffn_kernel(x_ref, g_ref, wg_ref, wu_ref, wd_ref, o_ref):
    o_ref[...] = _swiglu_half_step(x_ref[...], g_ref, wg_ref, wu_ref, wd_ref)


def _ffn(x, li, g, wg, wu, wd):
    t = x.shape[0]
    row = pl.BlockSpec((TOK_TILE, D_MODEL), lambda i: (i, 0))
    return pl.pallas_call(
        _ffn_kernel,
        grid=(t // TOK_TILE,),
        in_specs=[row, _layer(g, li), _layer(wg, li), _layer(wu, li), _layer(wd, li)],
        out_specs=row,
        out_shape=jax.ShapeDtypeStruct(x.shape, _F32),
        compiler_params=_params("parallel"),
        name="ffn",
    )(x, g, wg, wu, wd)


def _proj_kernel(h_ref, g_ref, wq_ref, wk_ref, wvt_ref, wlat_ref, gq_ref, gkv_ref,
                 wuqa_ref, wuqb_ref, wuk_ref, wuvt_ref, cos_ref, sin_ref,
                 sbq_ref, sbk_ref, sbvt_ref, mq_ref, mk_ref, mvt_ref):
    u = _rms(h_ref[...], g_ref[...]).astype(_BF)
    sb_scale = SB_HEAD_DIM ** -0.5 * math.log2(math.e)
    sbq_ref[...] = (jnp.dot(u, wq_ref[...], preferred_element_type=_F32) * sb_scale).astype(_BF)
    sbk_ref[...] = jnp.dot(u, wk_ref[...], preferred_element_type=_F32).astype(_BF)
    sbvt_ref[0, 0] = lax.dot_general(wvt_ref[...], u, _NT,
                                     preferred_element_type=_F32).astype(_BF)

    lat = jnp.dot(u, wlat_ref[...], preferred_element_type=_F32)
    cos = cos_ref[...]
    sin = sin_ref[...]
    cq = _rms(lat[:, :Q_LORA], gq_ref[...]).astype(_BF)
    ckv = _rms(lat[:, Q_LORA:Q_LORA + KV_LORA], gkv_ref[...]).astype(_BF)
    o_kr = Q_LORA + KV_LORA
    k_rope = lat[:, o_kr:o_kr + LANES] * cos + lat[:, o_kr + LANES:o_kr + 2 * LANES] * sin
    k_rope = k_rope.astype(_BF)

    qa = jnp.dot(cq, wuqa_ref[...], preferred_element_type=_F32)
    qb = jnp.dot(cq, wuqb_ref[...], preferred_element_type=_F32)
    kn = jnp.dot(ckv, wuk_ref[...], preferred_element_type=_F32)
    mvt_ref[0, 0] = lax.dot_general(wuvt_ref[...], ckv, _NT,
                                    preferred_element_type=_F32).astype(_BF)
    scale = MLA_QK_DIM ** -0.5 * math.log2(math.e)
    for hd in range(MLA_HEADS):
        o = hd * MLA_SLAB
        mq_ref[:, o:o + LANES] = (qa[:, o:o + LANES] * scale).astype(_BF)
        roped = qa[:, o + LANES:o + 2 * LANES] * cos + qb[:, hd * LANES:(hd + 1) * LANES] * sin
        mq_ref[:, o + LANES:o + 2 * LANES] = (roped * scale).astype(_BF)
        mk_ref[:, o:o + LANES] = kn[:, hd * LANES:(hd + 1) * LANES].astype(_BF)
        mk_ref[:, o + LANES:o + 2 * LANES] = k_rope


def _proj(h, g, w, cos2, sin2, batch, seq):
    t = batch * seq
    tm = ATT_TILE
    nblk = seq // tm
    row = lambda n: pl.BlockSpec((tm, n), lambda i: (i, 0))
    outs = (
        jax.ShapeDtypeStruct((t, SB_WIDTH), _BF),
        jax.ShapeDtypeStruct((t, SB_WIDTH), _BF),
        jax.ShapeDtypeStruct((batch, nblk, SB_WIDTH, tm), _BF),
        jax.ShapeDtypeStruct((t, MLA_HEADS * MLA_SLAB), _BF),
        jax.ShapeDtypeStruct((t, MLA_HEADS * MLA_SLAB), _BF),
        jax.ShapeDtypeStruct((batch, nblk, MLA_WIDTH, tm), _BF),
    )
    weights = (g, w["wq"], w["wk"], w["wvt"], w["wlat"], w["gq"], w["gkv"],
               w["wuqa"], w["wuqb"], w["wuk"], w["wuvt"])
    vt_spec = pl.BlockSpec((1, 1, SB_WIDTH, tm), lambda i: (i // nblk, i % nblk, 0, 0))
    rope_spec = pl.BlockSpec((tm, LANES), lambda i: (i % nblk, 0))
    return pl.pallas_call(
        _proj_kernel,
        grid=(t // tm,),
        in_specs=[row(D_MODEL)] + [_resident(a.shape) for a in weights] + [rope_spec, rope_spec],
        out_specs=(row(SB_WIDTH), row(SB_WIDTH), vt_spec,
                   row(MLA_HEADS * MLA_SLAB), row(MLA_HEADS * MLA_SLAB), vt_spec),
        out_shape=outs,
        compiler_params=_params("parallel"),
        name="mixer_proj",
    )(h, *weights, cos2, sin2)


def _sb_kernel(q_ref, k_ref, vt_ref, tneg_ref, o_ref, carry_sc, acc_sc):
    tb = ATT_TILE
    hd = SB_HEAD_DIM
    i = pl.program_id(1)
    lane = lax.broadcasted_iota(jnp.int32, (tb, LANES), 1)
    tneg = tneg_ref[...]
    key_row = lax.broadcasted_iota(jnp.int32, (tb, SB_HEADS * tb), 0)
    qry_col = lax.broadcasted_iota(jnp.int32, (tb, SB_HEADS * tb), 1) & (tb - 1)
    valid = key_row < qry_col

    carry_sc[...] = jnp.zeros(carry_sc.shape, _F32)
    acc_sc[...] = jnp.zeros(acc_sc.shape, _F32)

    q_pairs = []
    for pair in range(SB_HEADS // 2):
        q = q_ref[:, pair * LANES:(pair + 1) * LANES]
        zero = jnp.zeros_like(q)
        q_pairs.append(jnp.concatenate([jnp.where(lane < hd, q, zero),
                                        jnp.where(lane >= hd, q, zero)], axis=0))

    def all_heads(j, mask):
        rows = pl.ds(pl.multiple_of(j * tb, tb), tb)
        kbs = [k_ref[rows, pair * LANES:(pair + 1) * LANES] for pair in range(SB_HEADS // 2)]
        vbs = [vt_ref[0, j, h * hd:(h + 1) * hd, :] for h in range(SB_HEADS)]
        carry = carry_sc[...]
        z = jnp.concatenate([lax.dot_general(kb, qp, _NT, preferred_element_type=_F32)
                             for kb, qp in zip(kbs, q_pairs)], axis=1)
        sp = jnp.maximum(z, 0.0) + jnp.log2(1.0 + jnp.exp2(-jnp.abs(z)))
        if mask is not None:
            sp = jnp.where(mask, sp, 0.0)
        tail = jnp.dot(tneg, sp.astype(_BF), preferred_element_type=_F32)
        a = jnp.exp2((z - sp) + tail + carry)
        if mask is not None:
            a = jnp.where(mask, a, 0.0)
        ab = a.astype(_BF)
        contrib = jnp.concatenate(
            [jnp.dot(vbs[h], ab[:, h * tb:(h + 1) * tb], preferred_element_type=_F32)
             for h in range(SB_HEADS)], axis=0)
        carry_sc[...] = carry + tail[0:1, :] - sp[0:1, :]
        acc_sc[...] += contrib

    def alive():
        return (jnp.max(carry_sc[...]) > SB_DEAD_LOG2).astype(jnp.int32)

    all_heads(i, valid)

    def cond(state):
        j, live = state
        return jnp.logical_and(j >= 0, live > 0)

    def body(state):
        j, _ = state
        all_heads(j, None)
        return j - 1, alive()

    lax.while_loop(cond, body, (i - 1, alive()))
    o_ref[...] = acc_sc[...].T


def _sb_attention(sbq, sbk, sbvt, tneg, batch, seq):
    tb = ATT_TILE
    nblk = seq // tb
    return pl.pallas_call(
        _sb_kernel,
        grid=(batch, nblk),
        in_specs=[
            pl.BlockSpec((tb, SB_WIDTH), lambda b, i: (b * nblk + i, 0)),
            pl.BlockSpec((seq, SB_WIDTH), lambda b, i: (b, 0), pipeline_mode=pl.Buffered(1)),
            pl.BlockSpec((1, nblk, SB_WIDTH, tb), lambda b, i: (b, 0, 0, 0),
                         pipeline_mode=pl.Buffered(1)),
            _resident(tneg.shape),
        ],
        out_specs=pl.BlockSpec((tb, SB_WIDTH), lambda b, i: (b * nblk + i, 0)),
        out_shape=jax.ShapeDtypeStruct((batch * seq, SB_WIDTH), _F32),
        scratch_shapes=[pltpu.VMEM((1, SB_HEADS * tb), _F32), pltpu.VMEM((SB_WIDTH, tb), _F32)],
        compiler_params=_params("parallel", "arbitrary"),
        name="sb_attention",
    )(sbq, sbk, sbvt, tneg)


def _mla_kernel(q_ref, k_ref, vt_ref, o_ref, m_sc, l_sc, acc_sc, s_sc):
    tb = MLA_TILE
    sub = ATT_TILE
    pre = MLA_PREFETCH_HEADS
    i = pl.program_id(1)
    m_sc[...] = jnp.full(m_sc.shape, -1e30, _F32)
    l_sc[...] = jnp.zeros(l_sc.shape, _F32)
    acc_sc[...] = jnp.zeros(acc_sc.shape, _F32)
    key_row = lax.broadcasted_iota(jnp.int32, (tb, tb), 0)
    qry_col = lax.broadcasted_iota(jnp.int32, (tb, tb), 1)
    causal = key_row <= qry_col

    def key_block(j, h):
        rows = pl.ds(pl.multiple_of(j * tb, tb), tb)
        return k_ref[rows, h * MLA_SLAB:(h + 1) * MLA_SLAB]

    def scores(kb, h):
        q = q_ref[:, h * MLA_SLAB:(h + 1) * MLA_SLAB]
        return lax.dot_general(kb, q, _NT, preferred_element_type=_F32)

    for h in range(pre):
        s_sc[h] = scores(key_block(0, h), h)

    def all_heads(j, valid, prefetch_next):
        k_now = {h: key_block(j, h) for h in range(pre, MLA_HEADS)}
        k_next = {h: key_block(j + 1, h) for h in range(pre)} if prefetch_next else {}
        vts = {(h, c): vt_ref[0, j * (tb // sub) + c, h * MLA_V_DIM:(h + 1) * MLA_V_DIM, :]
               for h in range(MLA_HEADS) for c in range(tb // sub)}
        s_of = {h: s_sc[h] for h in range(pre)}
        s_next = {h: scores(kb, h) for h, kb in k_next.items()}

        def softmax_step(h, s):
            if valid is not None:
                s = jnp.where(valid, s, -jnp.inf)
            m_old = m_sc[h]
            m_new = jnp.maximum(m_old, jnp.max(s, axis=0, keepdims=True))
            alpha = jnp.exp2(m_old - m_new)
            p = jnp.exp2(s - m_new)
            l_sc[h] = alpha * l_sc[h] + jnp.sum(p, axis=0, keepdims=True)
            m_sc[h] = m_new
            return alpha, p.astype(_BF)

        def accumulate(h, alpha, pb):
            pv = None
            for c in range(tb // sub):
                part = jnp.dot(vts[h, c], pb[c * sub:(c + 1) * sub],
                               preferred_element_type=_F32)
                pv = part if pv is None else pv + part
            acc_sc[h] = alpha * acc_sc[h] + pv

        for h in range(pre, MLA_HEADS):
            s_of[h] = scores(k_now[h], h)
        for h in range(MLA_HEADS):
            accumulate(h, *softmax_step(h, s_of.pop(h)))
        for h, s in s_next.items():
            s_sc[h] = s

    def body(j, c):
        all_heads(j, None, True)
        return c

    lax.fori_loop(0, i, body, 0)
    all_heads(i, causal, False)
    out_t = jnp.concatenate([acc_sc[h] / l_sc[h] for h in range(MLA_HEADS)], axis=0)
    o_ref[...] = out_t.T


def _mla_attention(mq, mk, mvt, batch, seq):
    tb = MLA_TILE
    nblk = seq // tb
    return pl.pallas_call(
        _mla_kernel,
        grid=(batch, nblk),
        in_specs=[
            pl.BlockSpec((tb, MLA_HEADS * MLA_SLAB), lambda b, i: (b * nblk + i, 0)),
            pl.BlockSpec((seq, MLA_HEADS * MLA_SLAB), lambda b, i: (b, 0),
                         pipeline_mode=pl.Buffered(1)),
            pl.BlockSpec((1,) + mvt.shape[1:], lambda b, i: (b, 0, 0, 0),
                         pipeline_mode=pl.Buffered(1)),
        ],
        out_specs=pl.BlockSpec((tb, MLA_WIDTH), lambda b, i: (b * nblk + i, 0)),
        out_shape=jax.ShapeDtypeStruct((batch * seq, MLA_WIDTH), _F32),
        scratch_shapes=[pltpu.VMEM((MLA_HEADS, 1, tb), _F32), pltpu.VMEM((MLA_HEADS, 1, tb), _F32),
                        pltpu.VMEM((MLA_HEADS, MLA_V_DIM, tb), _F32),
                        pltpu.VMEM((MLA_PREFETCH_HEADS, tb, tb), _F32)],
        compiler_params=_params("parallel", "arbitrary"),
        name="mla_attention",
    )(mq, mk, mvt)


def _post_kernel(h_ref, sb_ref, mla_ref, p_ref, gsb_ref, gmla_ref, wout_ref,
                 gffn_ref, wg_ref, wu_ref, wd_ref, gple_ref, wpg_ref, wpp_ref, gfin_ref,
                 o_ref, *, final_norm):
    sb = _rms(sb_ref[...], gsb_ref[...]).astype(_BF)
    ml = _rms(mla_ref[...], gmla_ref[...]).astype(_BF)
    h1 = h_ref[...] + jnp.dot(sb, wout_ref[:SB_WIDTH, :], preferred_element_type=_F32)
    h1 = h1 + jnp.dot(ml, wout_ref[SB_WIDTH:, :], preferred_element_type=_F32)
    h2 = _swiglu_half_step(h1, gffn_ref, wg_ref, wu_ref, wd_ref)
    hn = _rms(h2, gple_ref[...]).astype(_BF)
    gate = jax.nn.sigmoid(jnp.dot(hn, wpg_ref[...], preferred_element_type=_F32))
    emb = jnp.dot(p_ref[...].astype(_BF), wpp_ref[...], preferred_element_type=_F32)
    out = h2 + gate * emb
    if final_norm:
        out = _rms(out, gfin_ref[...])
    o_ref[...] = out


def _post_mixer(h, sb, mla, p, li, gsb, gmla, wout, gffn, wg, wu, wd, gple, wpg, wpp, gfin,
                final_norm):
    t = h.shape[0]
    tm = POST_TILE
    row = lambda n: pl.BlockSpec((tm, n), lambda i: (i, 0))
    layered = (gsb, gmla, wout, gffn, wg, wu, wd, gple, wpg, wpp)
    return pl.pallas_call(
        functools.partial(_post_kernel, final_norm=final_norm),
        grid=(t // tm,),
        in_specs=[row(D_MODEL), row(SB_WIDTH), row(MLA_WIDTH),
                  pl.BlockSpec((None, tm, D_PLE), lambda i: (li, i, 0))]
                 + [_layer(a, li) for a in layered] + [_resident(gfin.shape)],
        out_specs=row(D_MODEL),
        out_shape=jax.ShapeDtypeStruct(h.shape, _F32),
        compiler_params=_params("parallel"),
        name="post_mixer",
    )(h, sb, mla, p, *layered, gfin)


def _swap_cols(w):
    half = w.shape[-1] // 2
    return jnp.concatenate([-w[..., half:], w[..., :half]], axis=-1)


def _pad_cols(w, n):
    return jnp.pad(w, ((0, 0), (0, n - w.shape[-1])))


def _mixer_weights(w_in, q_lat_norm, w_uq, kv_lat_norm, w_ukv):
    o1, o2, o3 = SB_WIDTH, 2 * SB_WIDTH, 3 * SB_WIDTH
    o4 = o3 + Q_LORA
    o5 = o4 + KV_LORA
    w_kr = w_in[:, o5:]
    wlat = jnp.concatenate([w_in[:, o3:o5], _pad_cols(w_kr, LANES),
                            _pad_cols(_swap_cols(w_kr), LANES)], axis=1)
    uq = w_uq.reshape(Q_LORA, MLA_HEADS, MLA_QK_DIM)
    rope_w = uq[:, :, MLA_NOPE_DIM:]
    wuqa = jnp.pad(uq, ((0, 0), (0, 0), (0, MLA_SLAB - MLA_QK_DIM)))
    wuqb = jnp.pad(_swap_cols(rope_w), ((0, 0), (0, 0), (0, LANES - MLA_ROPE_DIM)))
    ukv = w_ukv.reshape(KV_LORA, MLA_HEADS, MLA_NOPE_DIM + MLA_V_DIM)
    wuk = ukv[:, :, :MLA_NOPE_DIM].reshape(KV_LORA, -1)
    wuv = ukv[:, :, MLA_NOPE_DIM:].reshape(KV_LORA, -1)
    return {
        "wq": w_in[:, :o1].astype(_BF),
        "wk": w_in[:, o1:o2].astype(_BF),
        "wvt": w_in[:, o2:o3].T.astype(_BF),
        "wlat": wlat.astype(_BF),
        "gq": q_lat_norm.reshape(1, -1),
        "gkv": kv_lat_norm.reshape(1, -1),
        "wuqa": wuqa.reshape(Q_LORA, -1).astype(_BF),
        "wuqb": wuqb.reshape(Q_LORA, -1).astype(_BF),
        "wuk": wuk.astype(_BF),
        "wuvt": wuv.T.astype(_BF),
    }


def _rope_tables(seq):
    half = MLA_ROPE_DIM // 2
    inv_freq = ROPE_THETA ** (-jnp.arange(half, dtype=_F32) / half)
    ang = jnp.arange(seq, dtype=_F32)[:, None] * inv_freq[None, :]
    cos, sin = jnp.cos(ang), jnp.sin(ang)
    pad = jnp.zeros((seq, LANES - MLA_ROPE_DIM), _F32)
    return (jnp.concatenate([cos, cos, pad], axis=1), jnp.concatenate([sin, sin, pad], axis=1))


def kernel(x, p, ffn1_norm, ffn1_w_gate, ffn1_w_up, ffn1_w_down, mix_norm, w_in, q_lat_norm, w_uq, kv_lat_norm, w_ukv, sb_out_norm, mla_out_norm, w_out, ffn2_norm, ffn2_w_gate, ffn2_w_up, ffn2_w_down, ple_norm, w_ple_gate, w_ple_proj, final_norm):
    batch, seq, _ = x.shape
    depth = p.shape[0]
    assert seq % MLA_TILE == 0 and seq % ATT_TILE == 0 and (batch * seq) % TOK_TILE == 0
    t = batch * seq
    cos2, sin2 = _rope_tables(seq)
    idx = jnp.arange(ATT_TILE)
    tneg = jnp.where(idx[None, :] > idx[:, None], -1.0, 0.0).astype(_BF)
    vec = lambda v: v.reshape(1, -1)
    gains = lambda g: g.reshape(depth, 1, -1)
    p_tok = p.reshape(depth, t, D_PLE)
    w_out_bf, w_ple_gate_bf, w_ple_proj_bf = (w.astype(_BF) for w in (w_out, w_ple_gate, w_ple_proj))

    h = x.reshape(t, D_MODEL)
    for li in range(depth):
        h = _ffn(h, li, gains(ffn1_norm), ffn1_w_gate, ffn1_w_up, ffn1_w_down)
        mw = _mixer_weights(w_in[li], q_lat_norm[li], w_uq[li], kv_lat_norm[li], w_ukv[li])
        sbq, sbk, sbvt, mq, mk, mvt = _proj(h, vec(mix_norm[li]), mw, cos2, sin2, batch, seq)
        sb = _sb_attention(sbq, sbk, sbvt, tneg, batch, seq)
        mla = _mla_attention(mq, mk, mvt, batch, seq)
        h = _post_mixer(h, sb, mla, p_tok, li, gains(sb_out_norm), gains(mla_out_norm), w_out_bf,
                        gains(ffn2_norm), ffn2_w_gate, ffn2_w_up, ffn2_w_down,
                        gains(ple_norm), w_ple_gate_bf, w_ple_proj_bf, vec(final_norm),
                        li == depth - 1)
    return h.reshape(batch, seq, D_MODEL)
```

```python
import functools
import math

import jax
import jax.numpy as jnp
from jax import lax
from jax.experimental import pallas as pl
from jax.experimental.pallas import tpu as pltpu

D_MODEL = 1024
D_PLE = 256
SB_HEADS = 8
SB_HEAD_DIM = 64
SB_WIDTH = SB_HEADS * SB_HEAD_DIM
MLA_HEADS = 4
MLA_NOPE_DIM = 128
MLA_ROPE_DIM = 64
MLA_QK_DIM = MLA_NOPE_DIM + MLA_ROPE_DIM
MLA_V_DIM = 128
MLA_WIDTH = MLA_HEADS * MLA_V_DIM
Q_LORA = 256
KV_LORA = 128
D_FF = 2816
ROPE_THETA = 10000.0
EPS = 1e-6

LANES = 128
MXU_DIM = 256
VMEM_LIMIT_BYTES = 58 * 1024 * 1024

FF_CHUNK = MXU_DIM
N_FF_CHUNKS = D_FF // FF_CHUNK
TOK_TILE = 512
POST_TILE = 512
ATT_TILE = 256
MLA_TILE = 512
MLA_SLAB = 2 * LANES
SB_DEAD_LOG2 = -152.0
SB_MASKED_LOGIT = -1e30

_BF = jnp.bfloat16
_F32 = jnp.float32
_NT = (((1,), (1,)), ((), ()))


def _rms(x, g):
    return x * lax.rsqrt(jnp.mean(x * x, axis=-1, keepdims=True) + EPS) * g


def _params(*sem):
    return pltpu.CompilerParams(dimension_semantics=sem, vmem_limit_bytes=VMEM_LIMIT_BYTES)


def _resident(shape):
    zeros = (0,) * len(shape)
    return pl.BlockSpec(shape, lambda *_: zeros, pipeline_mode=pl.Buffered(1))


def _layer(arr, li):
    index = (li,) + (0,) * (arr.ndim - 1)
    return pl.BlockSpec((None,) + arr.shape[1:], lambda *_: index, pipeline_mode=pl.Buffered(1))


def _swiglu_half_step(x, g_ref, wg_ref, wu_ref, wd_ref):
    xn = _rms(x, g_ref[...]).astype(_BF)
    acc = jnp.zeros(x.shape, _F32)
    for c in range(N_FF_CHUNKS):
        cols = slice(c * FF_CHUNK, (c + 1) * FF_CHUNK)
        gate = jnp.dot(xn, wg_ref[:, cols].astype(_BF), preferred_element_type=_F32)
        up = jnp.dot(xn, wu_ref[:, cols].astype(_BF), preferred_element_type=_F32)
        hid = (gate * jax.nn.sigmoid(gate) * up).astype(_BF)
        acc = acc + jnp.dot(hid, wd_ref[cols, :].astype(_BF), preferred_element_type=_F32)
    return x + 0.5 * acc


def _ffn_kernel(x_ref, g_ref, wg_ref, wu_ref, wd_ref, o_ref):
    o_ref[...] = _swiglu_half_step(x_ref[...], g_ref, wg_ref, wu_ref, wd_ref)


def _ffn(x, li, g, wg, wu, wd):
    t = x.shape[0]
    row = pl.BlockSpec((TOK_TILE, D_MODEL), lambda i: (i, 0))
    return pl.pallas_call(
        _ffn_kernel,
        grid=(t // TOK_TILE,),
        in_specs=[row, _layer(g, li), _layer(wg, li), _layer(wu, li), _layer(wd, li)],
        out_specs=row,
        out_shape=jax.ShapeDtypeStruct(x.shape, _F32),
        compiler_params=_params("parallel"),
        name="ffn",
    )(x, g, wg, wu, wd)


def _proj_kernel(h_ref, g_ref, wq_ref, wk_ref, wvt_ref, wlat_ref, gq_ref, gkv_ref,
                 wuqa_ref, wuqb_ref, wuk_ref, wuvt_ref, cos_ref, sin_ref,
                 sbq_ref, sbk_ref, sbvt_ref, mq_ref, mk_ref, mvt_ref):
    u = _rms(h_ref[...], g_ref[...]).astype(_BF)
    sb_scale = SB_HEAD_DIM ** -0.5 * math.log2(math.e)
    sbq_ref[...] = (jnp.dot(u, wq_ref[...], preferred_element_type=_F32) * sb_scale).astype(_BF)
    sbk_ref[...] = jnp.dot(u, wk_ref[...], preferred_element_type=_F32).astype(_BF)
    sbvt = lax.dot_general(wvt_ref[...], u, _NT, preferred_element_type=_F32).astype(_BF)
    for c in range(sbvt_ref.shape[1]):
        sbvt_ref[0, c] = sbvt[:, c * ATT_TILE:(c + 1) * ATT_TILE]

    lat = jnp.dot(u, wlat_ref[...], preferred_element_type=_F32)
    cos = cos_ref[...]
    sin = sin_ref[...]
    cq = _rms(lat[:, :Q_LORA], gq_ref[...]).astype(_BF)
    ckv = _rms(lat[:, Q_LORA:Q_LORA + KV_LORA], gkv_ref[...]).astype(_BF)
    o_kr = Q_LORA + KV_LORA
    k_rope = lat[:, o_kr:o_kr + LANES] * cos + lat[:, o_kr + LANES:o_kr + 2 * LANES] * sin
    k_rope = k_rope.astype(_BF)

    qa = jnp.dot(cq, wuqa_ref[...], preferred_element_type=_F32)
    qb = jnp.dot(cq, wuqb_ref[...], preferred_element_type=_F32)
    kn = jnp.dot(ckv, wuk_ref[...], preferred_element_type=_F32)
    mvt = lax.dot_general(wuvt_ref[...], ckv, _NT, preferred_element_type=_F32).astype(_BF)
    for c in range(mvt_ref.shape[1]):
        mvt_ref[0, c] = mvt[:, c * ATT_TILE:(c + 1) * ATT_TILE]
    scale = MLA_QK_DIM ** -0.5 * math.log2(math.e)
    for hd in range(MLA_HEADS):
        o = hd * MLA_SLAB
        mq_ref[:, o:o + LANES] = (qa[:, o:o + LANES] * scale).astype(_BF)
        roped = qa[:, o + LANES:o + 2 * LANES] * cos + qb[:, hd * LANES:(hd + 1) * LANES] * sin
        mq_ref[:, o + LANES:o + 2 * LANES] = (roped * scale).astype(_BF)
        mk_ref[:, o:o + LANES] = kn[:, hd * LANES:(hd + 1) * LANES].astype(_BF)
        mk_ref[:, o + LANES:o + 2 * LANES] = k_rope


def _proj(h, g, w, cos2, sin2, batch, seq):
    t = batch * seq
    tm = TOK_TILE
    nblk = seq // tm
    kb = ATT_TILE
    row = lambda n: pl.BlockSpec((tm, n), lambda i: (i, 0))
    outs = (
        jax.ShapeDtypeStruct((t, SB_WIDTH), _BF),
        jax.ShapeDtypeStruct((t, SB_WIDTH), _BF),
        jax.ShapeDtypeStruct((batch, seq // kb, SB_WIDTH, kb), _BF),
        jax.ShapeDtypeStruct((t, MLA_HEADS * MLA_SLAB), _BF),
        jax.ShapeDtypeStruct((t, MLA_HEADS * MLA_SLAB), _BF),
        jax.ShapeDtypeStruct((batch, seq // kb, MLA_WIDTH, kb), _BF),
    )
    weights = (g, w["wq"], w["wk"], w["wvt"], w["wlat"], w["gq"], w["gkv"],
               w["wuqa"], w["wuqb"], w["wuk"], w["wuvt"])
    vt_spec = pl.BlockSpec((1, tm // kb, SB_WIDTH, kb), lambda i: (i // nblk, i % nblk, 0, 0))
    rope_spec = pl.BlockSpec((tm, LANES), lambda i: (i % nblk, 0))
    return pl.pallas_call(
        _proj_kernel,
        grid=(t // tm,),
        in_specs=[row(D_MODEL)] + [_resident(a.shape) for a in weights] + [rope_spec, rope_spec],
        out_specs=(row(SB_WIDTH), row(SB_WIDTH), vt_spec,
                   row(MLA_HEADS * MLA_SLAB), row(MLA_HEADS * MLA_SLAB), vt_spec),
        out_shape=outs,
        compiler_params=_params("parallel"),
        name="mixer_proj",
    )(h, *weights, cos2, sin2)


def _sb_kernel(q_ref, k_ref, vt_ref, tneg_ref, o_ref, carry_sc, acc_sc):
    tb = ATT_TILE
    hd = SB_HEAD_DIM
    i = pl.program_id(1)
    lane = lax.broadcasted_iota(jnp.int32, (tb, LANES), 1)
    tneg = tneg_ref[...]
    key_row = lax.broadcasted_iota(jnp.int32, (tb, SB_HEADS * tb), 0)
    qry_col = lax.broadcasted_iota(jnp.int32, (tb, SB_HEADS * tb), 1) & (tb - 1)
    valid = key_row < qry_col

    carry_sc[...] = jnp.zeros(carry_sc.shape, _F32)
    acc_sc[...] = jnp.zeros(acc_sc.shape, _F32)

    q_pairs = []
    for pair in range(SB_HEADS // 2):
        q = q_ref[:, pair * LANES:(pair + 1) * LANES]
        zero = jnp.zeros_like(q)
        q_pairs.append(jnp.concatenate([jnp.where(lane < hd, q, zero),
                                        jnp.where(lane >= hd, q, zero)], axis=0))

    def all_heads(j, mask):
        rows = pl.ds(pl.multiple_of(j * tb, tb), tb)
        kbs = [k_ref[rows, pair * LANES:(pair + 1) * LANES] for pair in range(SB_HEADS // 2)]
        vbs = [vt_ref[0, j, h * hd:(h + 1) * hd, :] for h in range(SB_HEADS)]
        carry = carry_sc[...]
        z = jnp.concatenate([lax.dot_general(kb, qp, _NT, preferred_element_type=_F32)
                             for kb, qp in zip(kbs, q_pairs)], axis=1)
        if mask is not None:
            z = jnp.where(mask, z, SB_MASKED_LOGIT)
        sp = jnp.maximum(z, 0.0) + jnp.log2(1.0 + jnp.exp2(-jnp.abs(z)))
        tail = jnp.dot(tneg, sp.astype(_BF), preferred_element_type=_F32)
        ab = jnp.exp2((z - sp) + tail + carry).astype(_BF)
        contrib = jnp.concatenate(
            [jnp.dot(vbs[h], ab[:, h * tb:(h + 1) * tb], preferred_element_type=_F32)
             for h in range(SB_HEADS)], axis=0)
        carry_sc[...] = carry + tail[0:1, :] - sp[0:1, :]
        acc_sc[...] += contrib

    def alive():
        return (jnp.max(carry_sc[...]) > SB_DEAD_LOG2).astype(jnp.int32)

    all_heads(i, valid)

    def cond(state):
        j, live = state
        return jnp.logical_and(j >= 0, live > 0)

    def body(state):
        j, _ = state
        all_heads(j, None)
        return j - 1, alive()

    lax.while_loop(cond, body, (i - 1, alive()))
    o_ref[...] = acc_sc[...].T


def _sb_attention(sbq, sbk, sbvt, tneg, batch, seq):
    tb = ATT_TILE
    nblk = seq // tb
    return pl.pallas_call(
        _sb_kernel,
        grid=(batch, nblk),
        in_specs=[
            pl.BlockSpec((tb, SB_WIDTH), lambda b, i: (b * nblk + i, 0)),
            pl.BlockSpec((seq, SB_WIDTH), lambda b, i: (b, 0), pipeline_mode=pl.Buffered(1)),
            pl.BlockSpec((1, nblk, SB_WIDTH, tb), lambda b, i: (b, 0, 0, 0),
                         pipeline_mode=pl.Buffered(1)),
            _resident(tneg.shape),
        ],
        out_specs=pl.BlockSpec((tb, SB_WIDTH), lambda b, i: (b * nblk + i, 0)),
        out_shape=jax.ShapeDtypeStruct((batch * seq, SB_WIDTH), _F32),
        scratch_shapes=[pltpu.VMEM((1, SB_HEADS * tb), _F32), pltpu.VMEM((SB_WIDTH, tb), _F32)],
        compiler_params=_params("parallel", "arbitrary"),
        name="sb_attention",
    )(sbq, sbk, sbvt, tneg)


def _mla_kernel(q_ref, k_ref, vt_ref, o_ref, m_sc, l_sc, acc_sc, s_sc, smax_sc):
    tb = MLA_TILE
    sub = ATT_TILE
    i = pl.program_id(1)
    m_sc[...] = jnp.full(m_sc.shape, -1e30, _F32)
    l_sc[...] = jnp.zeros(l_sc.shape, _F32)
    acc_sc[...] = jnp.zeros(acc_sc.shape, _F32)
    key_row = lax.broadcasted_iota(jnp.int32, (tb, tb), 0)
    qry_col = lax.broadcasted_iota(jnp.int32, (tb, tb), 1)
    causal = key_row <= qry_col

    def scores(j, h):
        rows = pl.ds(pl.multiple_of(j * tb, tb), tb)
        kb = k_ref[rows, h * MLA_SLAB:(h + 1) * MLA_SLAB]
        q = q_ref[:, h * MLA_SLAB:(h + 1) * MLA_SLAB]
        return lax.dot_general(kb, q, _NT, preferred_element_type=_F32)

    for h in range(MLA_HEADS):
        s = scores(0, h)
        s_sc[h] = s
        smax_sc[h] = jnp.max(s, axis=0, keepdims=True)

    def all_heads(j, valid, prefetch_next):
        s_next = [scores(j + 1, h) for h in range(MLA_HEADS)] if prefetch_next else []
        new_m, new_l, new_acc = [], [], []
        for h in range(MLA_HEADS):
            m_old = m_sc[h]
            if valid is None:
                m_new = jnp.maximum(m_old, smax_sc[h])
                p = jnp.exp2(s_sc[h] - m_new)
            else:
                s = jnp.where(valid, s_sc[h], -jnp.inf)
                m_new = jnp.maximum(m_old, jnp.max(s, axis=0, keepdims=True))
                p = jnp.exp2(s - m_new)
            alpha = jnp.exp2(m_old - m_new)
            new_l.append(alpha * l_sc[h] + jnp.sum(p, axis=0, keepdims=True))
            new_m.append(m_new)
            pb = p.astype(_BF)
            pv = None
            for c in range(tb // sub):
                vt = vt_ref[0, j * (tb // sub) + c, h * MLA_V_DIM:(h + 1) * MLA_V_DIM, :]
                part = jnp.dot(vt, pb[c * sub:(c + 1) * sub], preferred_element_type=_F32)
                pv = part if pv is None else pv + part
            new_acc.append(alpha * acc_sc[h] + pv)
        for h in range(MLA_HEADS):
            m_sc[h] = new_m[h]
            l_sc[h] = new_l[h]
            acc_sc[h] = new_acc[h]
        for h, s in enumerate(s_next):
            s_sc[h] = s
            smax_sc[h] = jnp.max(s, axis=0, keepdims=True)

    def body(j, c):
        all_heads(j, None, True)
        return c

    lax.fori_loop(0, i, body, 0)
    all_heads(i, causal, False)
    out_t = jnp.concatenate([acc_sc[h] / l_sc[h] for h in range(MLA_HEADS)], axis=0)
    o_ref[...] = out_t.T


def _mla_attention(mq, mk, mvt, batch, seq):
    tb = MLA_TILE
    nblk = seq // tb
    return pl.pallas_call(
        _mla_kernel,
        grid=(batch, nblk),
        in_specs=[
            pl.BlockSpec((tb, MLA_HEADS * MLA_SLAB), lambda b, i: (b * nblk + i, 0)),
            pl.BlockSpec((seq, MLA_HEADS * MLA_SLAB), lambda b, i: (b, 0),
                         pipeline_mode=pl.Buffered(1)),
            pl.BlockSpec((1,) + mvt.shape[1:], lambda b, i: (b, 0, 0, 0),
                         pipeline_mode=pl.Buffered(1)),
        ],
        out_specs=pl.BlockSpec((tb, MLA_WIDTH), lambda b, i: (b * nblk + i, 0)),
        out_shape=jax.ShapeDtypeStruct((batch * seq, MLA_WIDTH), _F32),
        scratch_shapes=[pltpu.VMEM((MLA_HEADS, 1, tb), _F32), pltpu.VMEM((MLA_HEADS, 1, tb), _F32),
                        pltpu.VMEM((MLA_HEADS, MLA_V_DIM, tb), _F32),
                        pltpu.VMEM((MLA_HEADS, tb, tb), _F32),
                        pltpu.VMEM((MLA_HEADS, 1, tb), _F32)],
        compiler_params=_params("parallel", "arbitrary"),
        name="mla_attention",
    )(mq, mk, mvt)


def _post_kernel(h_ref, sb_ref, mla_ref, p_ref, gsb_ref, gmla_ref, wout_ref,
                 gffn_ref, wg_ref, wu_ref, wd_ref, gple_ref, wpg_ref, wpp_ref, gfin_ref,
                 o_ref, *, final_norm):
    sb = _rms(sb_ref[...], gsb_ref[...]).astype(_BF)
    ml = _rms(mla_ref[...], gmla_ref[...]).astype(_BF)
    h1 = h_ref[...] + jnp.dot(sb, wout_ref[:SB_WIDTH, :], preferred_element_type=_F32)
    h1 = h1 + jnp.dot(ml, wout_ref[SB_WIDTH:, :], preferred_element_type=_F32)
    h2 = _swiglu_half_step(h1, gffn_ref, wg_ref, wu_ref, wd_ref)
    hn = _rms(h2, gple_ref[...]).astype(_BF)
    gate = jax.nn.sigmoid(jnp.dot(hn, wpg_ref[...], preferred_element_type=_F32))
    emb = jnp.dot(p_ref[...].astype(_BF), wpp_ref[...], preferred_element_type=_F32)
    out = h2 + gate * emb
    if final_norm:
        out = _rms(out, gfin_ref[...])
    o_ref[...] = out


def _post_mixer(h, sb, mla, p, li, gsb, gmla, wout, gffn, wg, wu, wd, gple, wpg, wpp, gfin,
                final_norm):
    t = h.shape[0]
    tm = POST_TILE
    row = lambda n: pl.BlockSpec((tm, n), lambda i: (i, 0))
    layered = (gsb, gmla, wout, gffn, wg, wu, wd, gple, wpg, wpp)
    return pl.pallas_call(
        functools.partial(_post_kernel, final_norm=final_norm),
        grid=(t // tm,),
        in_specs=[row(D_MODEL), row(SB_WIDTH), row(MLA_WIDTH),
                  pl.BlockSpec((None, tm, D_PLE), lambda i: (li, i, 0))]
                 + [_layer(a, li) for a in layered] + [_resident(gfin.shape)],
        out_specs=row(D_MODEL),
        out_shape=jax.ShapeDtypeStruct(h.shape, _F32),
        compiler_params=_params("parallel"),
        name="post_mixer",
    )(h, sb, mla, p, *layered, gfin)


def _swap_cols(w):
    half = w.shape[-1] // 2
    return jnp.concatenate([-w[..., half:], w[..., :half]], axis=-1)


def _pad_cols(w, n):
    return jnp.pad(w, ((0, 0), (0, n - w.shape[-1])))


def _mixer_weights(w_in, q_lat_norm, w_uq, kv_lat_norm, w_ukv):
    o1, o2, o3 = SB_WIDTH, 2 * SB_WIDTH, 3 * SB_WIDTH
    o4 = o3 + Q_LORA
    o5 = o4 + KV_LORA
    w_kr = w_in[:, o5:]
    wlat = jnp.concatenate([w_in[:, o3:o5], _pad_cols(w_kr, LANES),
                            _pad_cols(_swap_cols(w_kr), LANES)], axis=1)
    uq = w_uq.reshape(Q_LORA, MLA_HEADS, MLA_QK_DIM)
    rope_w = uq[:, :, MLA_NOPE_DIM:]
    wuqa = jnp.pad(uq, ((0, 0), (0, 0), (0, MLA_SLAB - MLA_QK_DIM)))
    wuqb = jnp.pad(_swap_cols(rope_w), ((0, 0), (0, 0), (0, LANES - MLA_ROPE_DIM)))
    ukv = w_ukv.reshape(KV_LORA, MLA_HEADS, MLA_NOPE_DIM + MLA_V_DIM)
    wuk = ukv[:, :, :MLA_NOPE_DIM].reshape(KV_LORA, -1)
    return {
        "wq": w_in[:, :o1].astype(_BF),
        "wk": w_in[:, o1:o2].astype(_BF),
        "wvt": w_in[:, o2:o3].T.astype(_BF),
        "wlat": wlat.astype(_BF),
        "gq": q_lat_norm.reshape(1, -1),
        "gkv": kv_lat_norm.reshape(1, -1),
        "wuqa": wuqa.reshape(Q_LORA, -1).astype(_BF),
        "wuqb": wuqb.reshape(Q_LORA, -1).astype(_BF),
        "wuk": wuk.astype(_BF),
        "wuvt": ukv[:, :, MLA_NOPE_DIM:].reshape(KV_LORA, -1).T.astype(_BF),
    }


def _rope_tables(seq):
    half = MLA_ROPE_DIM // 2
    inv_freq = ROPE_THETA ** (-jnp.arange(half, dtype=_F32) / half)
    ang = jnp.arange(seq, dtype=_F32)[:, None] * inv_freq[None, :]
    cos, sin = jnp.cos(ang), jnp.sin(ang)
    pad = jnp.zeros((seq, LANES - MLA_ROPE_DIM), _F32)
    return (jnp.concatenate([cos, cos, pad], axis=1), jnp.concatenate([sin, sin, pad], axis=1))


def kernel(x, p, ffn1_norm, ffn1_w_gate, ffn1_w_up, ffn1_w_down, mix_norm, w_in, q_lat_norm, w_uq, kv_lat_norm, w_ukv, sb_out_norm, mla_out_norm, w_out, ffn2_norm, ffn2_w_gate, ffn2_w_up, ffn2_w_down, ple_norm, w_ple_gate, w_ple_proj, final_norm):
    batch, seq, _ = x.shape
    depth = p.shape[0]
    assert seq % MLA_TILE == 0 and seq % ATT_TILE == 0 and (batch * seq) % TOK_TILE == 0
    t = batch * seq
    cos2, sin2 = _rope_tables(seq)
    idx = jnp.arange(ATT_TILE)
    tneg = jnp.where(idx[None, :] > idx[:, None], -1.0, 0.0).astype(_BF)
    vec = lambda v: v.reshape(1, -1)
    gains = lambda g: g.reshape(depth, 1, -1)
    p_tok = p.reshape(depth, t, D_PLE)
    w_out_bf, w_ple_gate_bf, w_ple_proj_bf = (w.astype(_BF) for w in (w_out, w_ple_gate, w_ple_proj))

    h = x.reshape(t, D_MODEL)
    for li in range(depth):
        h = _ffn(h, li, gains(ffn1_norm), ffn1_w_gate, ffn1_w_up, ffn1_w_down)
        mw = _mixer_weights(w_in[li], q_lat_norm[li], w_uq[li], kv_lat_norm[li], w_ukv[li])
        sbq, sbk, sbvt, mq, mk, mvt = _proj(h, vec(mix_norm[li]), mw, cos2, sin2, batch, seq)
        sb = _sb_attention(sbq, sbk, sbvt, tneg, batch, seq)
        mla = _mla_attention(mq, mk, mvt, batch, seq)
        h = _post_mixer(h, sb, mla, p_tok, li, gains(sb_out_norm), gains(mla_out_norm), w_out_bf,
                        gains(ffn2_norm), ffn2_w_gate, ffn2_w_up, ffn2_w_down,
                        gains(ple_norm), w_ple_gate_bf, w_ple_proj_bf, vec(final_norm),
                        li == depth - 1)
    return h.reshape(batch, seq, D_MODEL)
```

```python
import functools
import math

import jax
import jax.numpy as jnp
from jax import lax
from jax.experimental import pallas as pl
from jax.experimental.pallas import tpu as pltpu

D_MODEL = 1024
D_PLE = 256
SB_HEADS = 8
SB_HEAD_DIM = 64
SB_WIDTH = SB_HEADS * SB_HEAD_DIM
MLA_HEADS = 4
MLA_NOPE_DIM = 128
MLA_ROPE_DIM = 64
MLA_QK_DIM = MLA_NOPE_DIM + MLA_ROPE_DIM
MLA_V_DIM = 128
MLA_WIDTH = MLA_HEADS * MLA_V_DIM
Q_LORA = 256
KV_LORA = 128
D_FF = 2816
ROPE_THETA = 10000.0
EPS = 1e-6

LANES = 128
MXU_DIM = 256
VMEM_LIMIT_BYTES = 58 * 1024 * 1024

FF_CHUNK = MXU_DIM
N_FF_CHUNKS = D_FF // FF_CHUNK
TOK_TILE = 512
POST_TILE = 512
ATT_TILE = 256
MLA_TILE = 512
MLA_SLAB = 2 * LANES
SB_DEAD_LOG2 = -152.0
SB_MASKED_LOGIT = -1e30

_BF = jnp.bfloat16
_F32 = jnp.float32
_NT = (((1,), (1,)), ((), ()))


def _rms(x, g):
    return x * lax.rsqrt(jnp.mean(x * x, axis=-1, keepdims=True) + EPS) * g


def _params(*sem):
    return pltpu.CompilerParams(dimension_semantics=sem, vmem_limit_bytes=VMEM_LIMIT_BYTES)


def _resident(shape):
    zeros = (0,) * len(shape)
    return pl.BlockSpec(shape, lambda *_: zeros, pipeline_mode=pl.Buffered(1))


def _layer(arr, li):
    index = (li,) + (0,) * (arr.ndim - 1)
    return pl.BlockSpec((None,) + arr.shape[1:], lambda *_: index, pipeline_mode=pl.Buffered(1))


def _swiglu_half_step(x, g_ref, wg_ref, wu_ref, wd_ref):
    xn = _rms(x, g_ref[...]).astype(_BF)
    acc = jnp.zeros(x.shape, _F32)
    for c in range(N_FF_CHUNKS):
        cols = slice(c * FF_CHUNK, (c + 1) * FF_CHUNK)
        gate = jnp.dot(xn, wg_ref[:, cols].astype(_BF), preferred_element_type=_F32)
        up = jnp.dot(xn, wu_ref[:, cols].astype(_BF), preferred_element_type=_F32)
        hid = (gate * jax.nn.sigmoid(gate) * up).astype(_BF)
        acc = acc + jnp.dot(hid, wd_ref[cols, :].astype(_BF), preferred_element_type=_F32)
    return x + 0.5 * acc


def _ffn_kernel(x_ref, g_ref, wg_ref, wu_ref, wd_ref, o_ref):
    o_ref[...] = _swiglu_half_step(x_ref[...], g_ref, wg_ref, wu_ref, wd_ref)


def _ffn(x, li, g, wg, wu, wd):
    t = x.shape[0]
    row = pl.BlockSpec((TOK_TILE, D_MODEL), lambda i: (i, 0))
    return pl.pallas_call(
        _ffn_kernel,
        grid=(t // TOK_TILE,),
        in_specs=[row, _layer(g, li), _layer(wg, li), _layer(wu, li), _layer(wd, li)],
        out_specs=row,
        out_shape=jax.ShapeDtypeStruct(x.shape, _F32),
        compiler_params=_params("parallel"),
        name="ffn",
    )(x, g, wg, wu, wd)


def _rope_slab(x, cos, sin):
    return x * cos + pltpu.roll(x, LANES // 2, axis=1) * sin


def _proj_kernel(h_ref, g_ref, wq_ref, wk_ref, wvt_ref, wlat_ref, gq_ref, gkv_ref,
                 wuq_ref, wuk_ref, wuvt_ref, cos_ref, sin_ref,
                 sbq_ref, sbk_ref, sbvt_ref, mq_ref, mk_ref, mvt_ref):
    u = _rms(h_ref[...], g_ref[...]).astype(_BF)
    sb_scale = SB_HEAD_DIM ** -0.5 * math.log2(math.e)
    sbq_ref[...] = (jnp.dot(u, wq_ref[...], preferred_element_type=_F32) * sb_scale).astype(_BF)
    sbk_ref[...] = jnp.dot(u, wk_ref[...], preferred_element_type=_F32).astype(_BF)
    sbvt = lax.dot_general(wvt_ref[...], u, _NT, preferred_element_type=_F32).astype(_BF)
    for c in range(sbvt_ref.shape[1]):
        sbvt_ref[0, c] = sbvt[:, c * ATT_TILE:(c + 1) * ATT_TILE]

    lat = jnp.dot(u, wlat_ref[...], preferred_element_type=_F32)
    cos = cos_ref[...]
    sin = sin_ref[...]
    cq = _rms(lat[:, :Q_LORA], gq_ref[...]).astype(_BF)
    ckv = _rms(lat[:, Q_LORA:Q_LORA + KV_LORA], gkv_ref[...]).astype(_BF)
    o_kr = Q_LORA + KV_LORA
    k_rope = _rope_slab(lat[:, o_kr:o_kr + LANES], cos, sin).astype(_BF)

    qa = jnp.dot(cq, wuq_ref[...], preferred_element_type=_F32)
    kn = jnp.dot(ckv, wuk_ref[...], preferred_element_type=_F32)
    mvt = lax.dot_general(wuvt_ref[...], ckv, _NT, preferred_element_type=_F32).astype(_BF)
    for c in range(mvt_ref.shape[1]):
        mvt_ref[0, c] = mvt[:, c * ATT_TILE:(c + 1) * ATT_TILE]
    scale = MLA_QK_DIM ** -0.5 * math.log2(math.e)
    for hd in range(MLA_HEADS):
        o = hd * MLA_SLAB
        mq_ref[:, o:o + LANES] = (qa[:, o:o + LANES] * scale).astype(_BF)
        roped = _rope_slab(qa[:, o + LANES:o + 2 * LANES], cos, sin)
        mq_ref[:, o + LANES:o + 2 * LANES] = (roped * scale).astype(_BF)
        mk_ref[:, o:o + LANES] = kn[:, hd * LANES:(hd + 1) * LANES].astype(_BF)
        mk_ref[:, o + LANES:o + 2 * LANES] = k_rope


def _proj(h, g, w, cos2, sin2, batch, seq):
    t = batch * seq
    tm = TOK_TILE
    nblk = seq // tm
    kb = ATT_TILE
    row = lambda n: pl.BlockSpec((tm, n), lambda i: (i, 0))
    outs = (
        jax.ShapeDtypeStruct((t, SB_WIDTH), _BF),
        jax.ShapeDtypeStruct((t, SB_WIDTH), _BF),
        jax.ShapeDtypeStruct((batch, seq // kb, SB_WIDTH, kb), _BF),
        jax.ShapeDtypeStruct((t, MLA_HEADS * MLA_SLAB), _BF),
        jax.ShapeDtypeStruct((t, MLA_HEADS * MLA_SLAB), _BF),
        jax.ShapeDtypeStruct((batch, seq // kb, MLA_WIDTH, kb), _BF),
    )
    weights = (g, w["wq"], w["wk"], w["wvt"], w["wlat"], w["gq"], w["gkv"],
               w["wuq"], w["wuk"], w["wuvt"])
    vt_spec = pl.BlockSpec((1, tm // kb, SB_WIDTH, kb), lambda i: (i // nblk, i % nblk, 0, 0))
    rope_spec = pl.BlockSpec((tm, LANES), lambda i: (i % nblk, 0))
    return pl.pallas_call(
        _proj_kernel,
        grid=(t // tm,),
        in_specs=[row(D_MODEL)] + [_resident(a.shape) for a in weights] + [rope_spec, rope_spec],
        out_specs=(row(SB_WIDTH), row(SB_WIDTH), vt_spec,
                   row(MLA_HEADS * MLA_SLAB), row(MLA_HEADS * MLA_SLAB), vt_spec),
        out_shape=outs,
        compiler_params=_params("parallel"),
        name="mixer_proj",
    )(h, *weights, cos2, sin2)


def _sb_kernel(q_ref, k_ref, vt_ref, tneg_ref, o_ref, carry_sc, acc_sc):
    tb = ATT_TILE
    hd = SB_HEAD_DIM
    i = pl.program_id(1)
    lane = lax.broadcasted_iota(jnp.int32, (tb, LANES), 1)
    tneg = tneg_ref[...]
    key_row = lax.broadcasted_iota(jnp.int32, (tb, SB_HEADS * tb), 0)
    qry_col = lax.broadcasted_iota(jnp.int32, (tb, SB_HEADS * tb), 1) & (tb - 1)
    valid = key_row < qry_col

    carry_sc[...] = jnp.zeros(carry_sc.shape, _F32)
    acc_sc[...] = jnp.zeros(acc_sc.shape, _F32)

    q_pairs = []
    for pair in range(SB_HEADS // 2):
        q = q_ref[:, pair * LANES:(pair + 1) * LANES]
        zero = jnp.zeros_like(q)
        q_pairs.append(jnp.concatenate([jnp.where(lane < hd, q, zero),
                                        jnp.where(lane >= hd, q, zero)], axis=0))

    def key_blocks(blocks):
        operands = []
        for j, _ in blocks:
            rows = pl.ds(pl.multiple_of(j * tb, tb), tb)
            operands.append((
                [k_ref[rows, pair * LANES:(pair + 1) * LANES] for pair in range(SB_HEADS // 2)],
                [vt_ref[0, j, h * hd:(h + 1) * hd, :] for h in range(SB_HEADS)]))
        carry = carry_sc[...]
        total = None
        for (_, mask), (kbs, vbs) in zip(blocks, operands):
            z = jnp.concatenate([lax.dot_general(kb, qp, _NT, preferred_element_type=_F32)
                                 for kb, qp in zip(kbs, q_pairs)], axis=1)
            if mask is not None:
                z = jnp.where(mask, z, SB_MASKED_LOGIT)
            sp = jnp.maximum(z, 0.0) + jnp.log2(1.0 + jnp.exp2(-jnp.abs(z)))
            tail = jnp.dot(tneg, sp.astype(_BF), preferred_element_type=_F32)
            ab = jnp.exp2((z - sp) + tail + carry).astype(_BF)
            contrib = jnp.concatenate(
                [jnp.dot(vbs[h], ab[:, h * tb:(h + 1) * tb], preferred_element_type=_F32)
                 for h in range(SB_HEADS)], axis=0)
            carry = carry + tail[0:1, :] - sp[0:1, :]
            total = contrib if total is None else total + contrib
        carry_sc[...] = carry
        acc_sc[...] += total

    def alive():
        return (jnp.max(carry_sc[...]) > SB_DEAD_LOG2).astype(jnp.int32)

    @pl.when(i == 0)
    def _():
        key_blocks([(i, valid)])

    @pl.when(i > 0)
    def _():
        key_blocks([(i, valid), (i - 1, None)])

    def cond(state):
        j, live = state
        return jnp.logical_and(j >= 0, live > 0)

    def body(state):
        j, _ = state
        key_blocks([(j, None)])
        return j - 1, alive()

    lax.while_loop(cond, body, (i - 2, alive()))
    o_ref[...] = acc_sc[...].T


def _sb_attention(sbq, sbk, sbvt, tneg, batch, seq):
    tb = ATT_TILE
    nblk = seq // tb
    return pl.pallas_call(
        _sb_kernel,
        grid=(batch, nblk),
        in_specs=[
            pl.BlockSpec((tb, SB_WIDTH), lambda b, i: (b * nblk + i, 0)),
            pl.BlockSpec((seq, SB_WIDTH), lambda b, i: (b, 0), pipeline_mode=pl.Buffered(1)),
            pl.BlockSpec((1, nblk, SB_WIDTH, tb), lambda b, i: (b, 0, 0, 0),
                         pipeline_mode=pl.Buffered(1)),
            _resident(tneg.shape),
        ],
        out_specs=pl.BlockSpec((tb, SB_WIDTH), lambda b, i: (b * nblk + i, 0)),
        out_shape=jax.ShapeDtypeStruct((batch * seq, SB_WIDTH), _F32),
        scratch_shapes=[pltpu.VMEM((1, SB_HEADS * tb), _F32), pltpu.VMEM((SB_WIDTH, tb), _F32)],
        compiler_params=_params("parallel", "arbitrary"),
        name="sb_attention",
    )(sbq, sbk, sbvt, tneg)


def _mla_kernel(q_ref, k_ref, vt_ref, o_ref, m_sc, l_sc, acc_sc, s_sc, smax_sc, p_sc):
    tb = MLA_TILE
    sub = ATT_TILE
    i = pl.program_id(1)
    m_sc[...] = jnp.full(m_sc.shape, -1e30, _F32)
    l_sc[...] = jnp.zeros(l_sc.shape, _F32)
    acc_sc[...] = jnp.zeros(acc_sc.shape, _F32)
    key_row = lax.broadcasted_iota(jnp.int32, (tb, tb), 0)
    qry_col = lax.broadcasted_iota(jnp.int32, (tb, tb), 1)
    causal = key_row <= qry_col

    def scores(j, h):
        rows = pl.ds(pl.multiple_of(j * tb, tb), tb)
        kb = k_ref[rows, h * MLA_SLAB:(h + 1) * MLA_SLAB]
        q = q_ref[:, h * MLA_SLAB:(h + 1) * MLA_SLAB]
        return lax.dot_general(kb, q, _NT, preferred_element_type=_F32)

    for h in range(MLA_HEADS):
        s = scores(0, h)
        s_sc[h] = s
        smax_sc[h] = jnp.max(s, axis=0, keepdims=True)

    def all_heads(j, valid, prefetch_next):
        s_next = [scores(j + 1, h) for h in range(MLA_HEADS)] if prefetch_next else []
        new_m, new_l, new_acc = [], [], []
        for h in range(MLA_HEADS):
            m_old = m_sc[h]
            if valid is None:
                m_new = jnp.maximum(m_old, smax_sc[h])
                p = jnp.exp2(s_sc[h] - m_new)
            else:
                s = jnp.where(valid, s_sc[h], -jnp.inf)
                m_new = jnp.maximum(m_old, jnp.max(s, axis=0, keepdims=True))
                p = jnp.exp2(s - m_new)
            alpha = jnp.exp2(m_old - m_new)
            new_l.append(alpha * l_sc[h] + jnp.sum(p, axis=0, keepdims=True))
            new_m.append(m_new)
            p_sc[h] = p.astype(_BF)
            pv = None
            for c in range(tb // sub):
                vt = vt_ref[0, j * (tb // sub) + c, h * MLA_V_DIM:(h + 1) * MLA_V_DIM, :]
                part = jnp.dot(vt, p_sc[h, c * sub:(c + 1) * sub, :], preferred_element_type=_F32)
                pv = part if pv is None else pv + part
            new_acc.append(alpha * acc_sc[h] + pv)
        for h in range(MLA_HEADS):
            m_sc[h] = new_m[h]
            l_sc[h] = new_l[h]
            acc_sc[h] = new_acc[h]
        for h, s in enumerate(s_next):
            s_sc[h] = s
            smax_sc[h] = jnp.max(s, axis=0, keepdims=True)

    def body(j, c):
        all_heads(j, None, True)
        return c

    lax.fori_loop(0, i, body, 0)
    all_heads(i, causal, False)
    out_t = jnp.concatenate([acc_sc[h] / l_sc[h] for h in range(MLA_HEADS)], axis=0)
    o_ref[...] = out_t.T


def _mla_attention(mq, mk, mvt, batch, seq):
    tb = MLA_TILE
    nblk = seq // tb
    return pl.pallas_call(
        _mla_kernel,
        grid=(batch, nblk),
        in_specs=[
            pl.BlockSpec((tb, MLA_HEADS * MLA_SLAB), lambda b, i: (b * nblk + i, 0)),
            pl.BlockSpec((seq, MLA_HEADS * MLA_SLAB), lambda b, i: (b, 0),
                         pipeline_mode=pl.Buffered(1)),
            pl.BlockSpec((1,) + mvt.shape[1:], lambda b, i: (b, 0, 0, 0),
                         pipeline_mode=pl.Buffered(1)),
        ],
        out_specs=pl.BlockSpec((tb, MLA_WIDTH), lambda b, i: (b * nblk + i, 0)),
        out_shape=jax.ShapeDtypeStruct((batch * seq, MLA_WIDTH), _F32),
        scratch_shapes=[pltpu.VMEM((MLA_HEADS, 1, tb), _F32), pltpu.VMEM((MLA_HEADS, 1, tb), _F32),
                        pltpu.VMEM((MLA_HEADS, MLA_V_DIM, tb), _F32),
                        pltpu.VMEM((MLA_HEADS, tb, tb), _F32),
                        pltpu.VMEM((MLA_HEADS, 1, tb), _F32),
                        pltpu.VMEM((MLA_HEADS, tb, tb), _BF)],
        compiler_params=_params("parallel", "arbitrary"),
        name="mla_attention",
    )(mq, mk, mvt)


def _post_kernel(h_ref, sb_ref, mla_ref, p_ref, gsb_ref, gmla_ref, wout_ref,
                 gffn_ref, wg_ref, wu_ref, wd_ref, gple_ref, wpg_ref, wpp_ref, gfin_ref,
                 o_ref, *, final_norm):
    sb = _rms(sb_ref[...], gsb_ref[...]).astype(_BF)
    ml = _rms(mla_ref[...], gmla_ref[...]).astype(_BF)
    h1 = h_ref[...] + jnp.dot(sb, wout_ref[:SB_WIDTH, :], preferred_element_type=_F32)
    h1 = h1 + jnp.dot(ml, wout_ref[SB_WIDTH:, :], preferred_element_type=_F32)
    h2 = _swiglu_half_step(h1, gffn_ref, wg_ref, wu_ref, wd_ref)
    hn = _rms(h2, gple_ref[...]).astype(_BF)
    gate = jax.nn.sigmoid(jnp.dot(hn, wpg_ref[...], preferred_element_type=_F32))
    emb = jnp.dot(p_ref[...].astype(_BF), wpp_ref[...], preferred_element_type=_F32)
    out = h2 + gate * emb
    if final_norm:
        out = _rms(out, gfin_ref[...])
    o_ref[...] = out


def _post_mixer(h, sb, mla, p, li, gsb, gmla, wout, gffn, wg, wu, wd, gple, wpg, wpp, gfin,
                final_norm):
    t = h.shape[0]
    tm = POST_TILE
    row = lambda n: pl.BlockSpec((tm, n), lambda i: (i, 0))
    layered = (gsb, gmla, wout, gffn, wg, wu, wd, gple, wpg, wpp)
    return pl.pallas_call(
        functools.partial(_post_kernel, final_norm=final_norm),
        grid=(t // tm,),
        in_specs=[row(D_MODEL), row(SB_WIDTH), row(MLA_WIDTH),
                  pl.BlockSpec((None, tm, D_PLE), lambda i: (li, i, 0))]
                 + [_layer(a, li) for a in layered] + [_resident(gfin.shape)],
        out_specs=row(D_MODEL),
        out_shape=jax.ShapeDtypeStruct(h.shape, _F32),
        compiler_params=_params("parallel"),
        name="post_mixer",
    )(h, sb, mla, p, *layered, gfin)


def _swap_cols(w):
    half = w.shape[-1] // 2
    return jnp.concatenate([-w[..., half:], w[..., :half]], axis=-1)


def _mixer_weights(w_in, q_lat_norm, w_uq, kv_lat_norm, w_ukv):
    o1, o2, o3 = SB_WIDTH, 2 * SB_WIDTH, 3 * SB_WIDTH
    o4 = o3 + Q_LORA
    o5 = o4 + KV_LORA
    w_kr = w_in[:, o5:]
    wlat = jnp.concatenate([w_in[:, o3:o5], w_kr, _swap_cols(w_kr)], axis=1)
    uq = w_uq.reshape(Q_LORA, MLA_HEADS, MLA_QK_DIM)
    wuq = jnp.concatenate([uq, _swap_cols(uq[:, :, MLA_NOPE_DIM:])], axis=-1)
    ukv = w_ukv.reshape(KV_LORA, MLA_HEADS, MLA_NOPE_DIM + MLA_V_DIM)
    wuk = ukv[:, :, :MLA_NOPE_DIM].reshape(KV_LORA, -1)
    return {
        "wq": w_in[:, :o1].astype(_BF),
        "wk": w_in[:, o1:o2].astype(_BF),
        "wvt": w_in[:, o2:o3].T.astype(_BF),
        "wlat": wlat.astype(_BF),
        "gq": q_lat_norm.reshape(1, -1),
        "gkv": kv_lat_norm.reshape(1, -1),
        "wuq": wuq.reshape(Q_LORA, -1).astype(_BF),
        "wuk": wuk.astype(_BF),
        "wuvt": ukv[:, :, MLA_NOPE_DIM:].reshape(KV_LORA, -1).T.astype(_BF),
    }


def _rope_tables(seq):
    half = MLA_ROPE_DIM // 2
    inv_freq = ROPE_THETA ** (-jnp.arange(half, dtype=_F32) / half)
    ang = jnp.arange(seq, dtype=_F32)[:, None] * inv_freq[None, :]
    cos, sin = jnp.cos(ang), jnp.sin(ang)
    pad = jnp.zeros((seq, LANES - MLA_ROPE_DIM), _F32)
    return (jnp.concatenate([cos, cos, pad], axis=1), jnp.concatenate([sin, sin, pad], axis=1))


def kernel(x, p, ffn1_norm, ffn1_w_gate, ffn1_w_up, ffn1_w_down, mix_norm, w_in, q_lat_norm, w_uq, kv_lat_norm, w_ukv, sb_out_norm, mla_out_norm, w_out, ffn2_norm, ffn2_w_gate, ffn2_w_up, ffn2_w_down, ple_norm, w_ple_gate, w_ple_proj, final_norm):
    batch, seq, _ = x.shape
    depth = p.shape[0]
    assert seq % MLA_TILE == 0 and seq % ATT_TILE == 0 and (batch * seq) % TOK_TILE == 0
    t = batch * seq
    cos2, sin2 = _rope_tables(seq)
    idx = jnp.arange(ATT_TILE)
    tneg = jnp.where(idx[None, :] > idx[:, None], -1.0, 0.0).astype(_BF)
    vec = lambda v: v.reshape(1, -1)
    gains = lambda g: g.reshape(depth, 1, -1)
    p_tok = p.reshape(depth, t, D_PLE)
    w_out_bf, w_ple_gate_bf, w_ple_proj_bf = (w.astype(_BF) for w in (w_out, w_ple_gate, w_ple_proj))

    h = x.reshape(t, D_MODEL)
    for li in range(depth):
        h = _ffn(h, li, gains(ffn1_norm), ffn1_w_gate, ffn1_w_up, ffn1_w_down)
        mw = _mixer_weights(w_in[li], q_lat_norm[li], w_uq[li], kv_lat_norm[li], w_ukv[li])
        sbq, sbk, sbvt, mq, mk, mvt = _proj(h, vec(mix_norm[li]), mw, cos2, sin2, batch, seq)
        sb = _sb_attention(sbq, sbk, sbvt, tneg, batch, seq)
        mla = _mla_attention(mq, mk, mvt, batch, seq)
        h = _post_mixer(h, sb, mla, p_tok, li, gains(sb_out_norm), gains(mla_out_norm), w_out_bf,
                        gains(ffn2_norm), ffn2_w_gate, ffn2_w_up, ffn2_w_down,
                        gains(ple_norm), w_ple_gate_bf, w_ple_proj_bf, vec(final_norm),
                        li == depth - 1)
    return h.reshape(batch, seq, D_MODEL)
```

```python
import functools
import math

import jax
import jax.numpy as jnp
from jax import lax
from jax.experimental import pallas as pl
from jax.experimental.pallas import tpu as pltpu

D_MODEL = 1024
D_PLE = 256
SB_HEADS = 8
SB_HEAD_DIM = 64
SB_WIDTH = SB_HEADS * SB_HEAD_DIM
MLA_HEADS = 4
MLA_NOPE_DIM = 128
MLA_ROPE_DIM = 64
MLA_QK_DIM = MLA_NOPE_DIM + MLA_ROPE_DIM
MLA_V_DIM = 128
MLA_WIDTH = MLA_HEADS * MLA_V_DIM
Q_LORA = 256
KV_LORA = 128
D_FF = 2816
ROPE_THETA = 10000.0
EPS = 1e-6

LANES = 128
MXU_DIM = 256
VMEM_LIMIT_BYTES = 58 * 1024 * 1024

FF_CHUNK = MXU_DIM
N_FF_CHUNKS = D_FF // FF_CHUNK
TOK_TILE = 512
POST_TILE = 512
ATT_TILE = 256
MLA_TILE = 512
MLA_SLAB = 2 * LANES
SB_DEAD_LOG2 = -152.0
SB_MASKED_LOGIT = -1e30

_BF = jnp.bfloat16
_F32 = jnp.float32
_NT = (((1,), (1,)), ((), ()))


def _rms(x, g):
    return x * lax.rsqrt(jnp.mean(x * x, axis=-1, keepdims=True) + EPS) * g


def _params(*sem):
    return pltpu.CompilerParams(dimension_semantics=sem, vmem_limit_bytes=VMEM_LIMIT_BYTES)


def _resident(shape):
    zeros = (0,) * len(shape)
    return pl.BlockSpec(shape, lambda *_: zeros, pipeline_mode=pl.Buffered(1))


def _layer(arr, li):
    index = (li,) + (0,) * (arr.ndim - 1)
    return pl.BlockSpec((None,) + arr.shape[1:], lambda *_: index, pipeline_mode=pl.Buffered(1))


def _swiglu_half_step(x, g_ref, wg_ref, wu_ref, wd_ref):
    xn = _rms(x, g_ref[...]).astype(_BF)
    acc = jnp.zeros(x.shape, _F32)
    for c in range(N_FF_CHUNKS):
        cols = slice(c * FF_CHUNK, (c + 1) * FF_CHUNK)
        gate = jnp.dot(xn, wg_ref[:, cols].astype(_BF), preferred_element_type=_F32)
        up = jnp.dot(xn, wu_ref[:, cols].astype(_BF), preferred_element_type=_F32)
        hid = (gate * jax.nn.sigmoid(gate) * up).astype(_BF)
        acc = acc + jnp.dot(hid, wd_ref[cols, :].astype(_BF), preferred_element_type=_F32)
    return x + 0.5 * acc


def _ffn_kernel(x_ref, g_ref, wg_ref, wu_ref, wd_ref, o_ref):
    o_ref[...] = _swiglu_half_step(x_ref[...], g_ref, wg_ref, wu_ref, wd_ref)


def _ffn(x, li, g, wg, wu, wd):
    t = x.shape[0]
    row = pl.BlockSpec((TOK_TILE, D_MODEL), lambda i: (i, 0))
    return pl.pallas_call(
        _ffn_kernel,
        grid=(t // TOK_TILE,),
        in_specs=[row, _layer(g, li), _layer(wg, li), _layer(wu, li), _layer(wd, li)],
        out_specs=row,
        out_shape=jax.ShapeDtypeStruct(x.shape, _F32),
        compiler_params=_params("parallel"),
        name="ffn",
    )(x, g, wg, wu, wd)


def _rope_slab(x, cos, sin):
    return x * cos + pltpu.roll(x, LANES // 2, axis=1) * sin


def _proj_kernel(h_ref, g_ref, wq_ref, wk_ref, wvt_ref, wlat_ref, gq_ref, gkv_ref,
                 wuq_ref, wuk_ref, wuvt_ref, cos_ref, sin_ref,
                 sbq_ref, sbk_ref, sbvt_ref, mq_ref, mk_ref, mvt_ref):
    u = _rms(h_ref[...], g_ref[...]).astype(_BF)
    sb_scale = SB_HEAD_DIM ** -0.5 * math.log2(math.e)
    sbq_ref[...] = (jnp.dot(u, wq_ref[...], preferred_element_type=_F32) * sb_scale).astype(_BF)
    sbk_ref[...] = jnp.dot(u, wk_ref[...], preferred_element_type=_F32).astype(_BF)
    sbvt = lax.dot_general(wvt_ref[...], u, _NT, preferred_element_type=_F32).astype(_BF)
    for c in range(sbvt_ref.shape[1]):
        sbvt_ref[0, c] = sbvt[:, c * ATT_TILE:(c + 1) * ATT_TILE]

    lat = jnp.dot(u, wlat_ref[...], preferred_element_type=_F32)
    cos = cos_ref[...]
    sin = sin_ref[...]
    cq = _rms(lat[:, :Q_LORA], gq_ref[...]).astype(_BF)
    ckv = _rms(lat[:, Q_LORA:Q_LORA + KV_LORA], gkv_ref[...]).astype(_BF)
    o_kr = Q_LORA + KV_LORA
    k_rope = _rope_slab(lat[:, o_kr:o_kr + LANES], cos, sin).astype(_BF)

    qa = jnp.dot(cq, wuq_ref[...], preferred_element_type=_F32)
    kn = jnp.dot(ckv, wuk_ref[...], preferred_element_type=_F32)
    mvt = lax.dot_general(wuvt_ref[...], ckv, _NT, preferred_element_type=_F32).astype(_BF)
    for c in range(mvt_ref.shape[1]):
        mvt_ref[0, c] = mvt[:, c * ATT_TILE:(c + 1) * ATT_TILE]
    scale = MLA_QK_DIM ** -0.5 * math.log2(math.e)
    for hd in range(MLA_HEADS):
        o = hd * MLA_SLAB
        mq_ref[:, o:o + LANES] = (qa[:, o:o + LANES] * scale).astype(_BF)
        roped = _rope_slab(qa[:, o + LANES:o + 2 * LANES], cos, sin)
        mq_ref[:, o + LANES:o + 2 * LANES] = (roped * scale).astype(_BF)
        mk_ref[:, o:o + LANES] = kn[:, hd * LANES:(hd + 1) * LANES].astype(_BF)
        mk_ref[:, o + LANES:o + 2 * LANES] = k_rope


def _proj(h, g, w, cos2, sin2, batch, seq):
    t = batch * seq
    tm = TOK_TILE
    nblk = seq // tm
    kb = ATT_TILE
    row = lambda n: pl.BlockSpec((tm, n), lambda i: (i, 0))
    outs = (
        jax.ShapeDtypeStruct((t, SB_WIDTH), _BF),
        jax.ShapeDtypeStruct((t, SB_WIDTH), _BF),
        jax.ShapeDtypeStruct((batch, seq // kb, SB_WIDTH, kb), _BF),
        jax.ShapeDtypeStruct((t, MLA_HEADS * MLA_SLAB), _BF),
        jax.ShapeDtypeStruct((t, MLA_HEADS * MLA_SLAB), _BF),
        jax.ShapeDtypeStruct((batch, seq // kb, MLA_WIDTH, kb), _BF),
    )
    weights = (g, w["wq"], w["wk"], w["wvt"], w["wlat"], w["gq"], w["gkv"],
               w["wuq"], w["wuk"], w["wuvt"])
    vt_spec = pl.BlockSpec((1, tm // kb, SB_WIDTH, kb), lambda i: (i // nblk, i % nblk, 0, 0))
    rope_spec = pl.BlockSpec((tm, LANES), lambda i: (i % nblk, 0))
    return pl.pallas_call(
        _proj_kernel,
        grid=(t // tm,),
        in_specs=[row(D_MODEL)] + [_resident(a.shape) for a in weights] + [rope_spec, rope_spec],
        out_specs=(row(SB_WIDTH), row(SB_WIDTH), vt_spec,
                   row(MLA_HEADS * MLA_SLAB), row(MLA_HEADS * MLA_SLAB), vt_spec),
        out_shape=outs,
        compiler_params=_params("parallel"),
        name="mixer_proj",
    )(h, *weights, cos2, sin2)


def _sb_kernel(q_ref, k_ref, vt_ref, tneg_ref, o_ref, carry_sc, acc_sc, spb_sc, lsig_sc, ab_sc):
    tb = ATT_TILE
    hd = SB_HEAD_DIM
    i = pl.program_id(1)
    lane = lax.broadcasted_iota(jnp.int32, (tb, LANES), 1)
    tneg = tneg_ref[...]
    key_row = lax.broadcasted_iota(jnp.int32, (tb, SB_HEADS * tb), 0)
    qry_col = lax.broadcasted_iota(jnp.int32, (tb, SB_HEADS * tb), 1) & (tb - 1)
    valid = key_row < qry_col

    carry_sc[...] = jnp.zeros(carry_sc.shape, _F32)
    acc_sc[...] = jnp.zeros(acc_sc.shape, _F32)

    q_pairs = []
    for pair in range(SB_HEADS // 2):
        q = q_ref[:, pair * LANES:(pair + 1) * LANES]
        zero = jnp.zeros_like(q)
        q_pairs.append(jnp.concatenate([jnp.where(lane < hd, q, zero),
                                        jnp.where(lane >= hd, q, zero)], axis=0))

    def key_blocks(blocks):
        operands = []
        for j, _ in blocks:
            rows = pl.ds(pl.multiple_of(j * tb, tb), tb)
            operands.append((
                [k_ref[rows, pair * LANES:(pair + 1) * LANES] for pair in range(SB_HEADS // 2)],
                [vt_ref[0, j, h * hd:(h + 1) * hd, :] for h in range(SB_HEADS)]))
        carry = carry_sc[...]
        total = None
        for slot, ((_, mask), (kbs, vbs)) in enumerate(zip(blocks, operands)):
            z = jnp.concatenate([lax.dot_general(kb, qp, _NT, preferred_element_type=_F32)
                                 for kb, qp in zip(kbs, q_pairs)], axis=1)
            if mask is not None:
                z = jnp.where(mask, z, SB_MASKED_LOGIT)
            sp = jnp.maximum(z, 0.0) + jnp.log2(1.0 + jnp.exp2(-jnp.abs(z)))
            spb_sc[slot] = sp.astype(_BF)
            lsig_sc[slot] = z - sp
            tail = jnp.dot(tneg, spb_sc[slot], preferred_element_type=_F32)
            ab_sc[slot] = jnp.exp2(lsig_sc[slot] + tail + carry).astype(_BF)
            contrib = jnp.concatenate(
                [jnp.dot(vbs[h], ab_sc[slot, :, h * tb:(h + 1) * tb], preferred_element_type=_F32)
                 for h in range(SB_HEADS)], axis=0)
            carry = carry + tail[0:1, :] - sp[0:1, :]
            total = contrib if total is None else total + contrib
        carry_sc[...] = carry
        acc_sc[...] += total

    def alive():
        return (jnp.max(carry_sc[...]) > SB_DEAD_LOG2).astype(jnp.int32)

    @pl.when(i == 0)
    def _():
        key_blocks([(i, valid)])

    @pl.when(i > 0)
    def _():
        key_blocks([(i, valid), (i - 1, None)])

    def cond(state):
        j, live = state
        return jnp.logical_and(j >= 0, live > 0)

    def body(state):
        j, _ = state
        key_blocks([(j, None)])
        return j - 1, alive()

    lax.while_loop(cond, body, (i - 2, alive()))
    o_ref[...] = acc_sc[...].T


def _sb_attention(sbq, sbk, sbvt, tneg, batch, seq):
    tb = ATT_TILE
    nblk = seq // tb
    return pl.pallas_call(
        _sb_kernel,
        grid=(batch, nblk),
        in_specs=[
            pl.BlockSpec((tb, SB_WIDTH), lambda b, i: (b * nblk + i, 0)),
            pl.BlockSpec((seq, SB_WIDTH), lambda b, i: (b, 0), pipeline_mode=pl.Buffered(1)),
            pl.BlockSpec((1, nblk, SB_WIDTH, tb), lambda b, i: (b, 0, 0, 0),
                         pipeline_mode=pl.Buffered(1)),
            _resident(tneg.shape),
        ],
        out_specs=pl.BlockSpec((tb, SB_WIDTH), lambda b, i: (b * nblk + i, 0)),
        out_shape=jax.ShapeDtypeStruct((batch * seq, SB_WIDTH), _F32),
        scratch_shapes=[pltpu.VMEM((1, SB_HEADS * tb), _F32), pltpu.VMEM((SB_WIDTH, tb), _F32),
                        pltpu.VMEM((2, tb, SB_HEADS * tb), _BF),
                        pltpu.VMEM((2, tb, SB_HEADS * tb), _F32),
                        pltpu.VMEM((2, tb, SB_HEADS * tb), _BF)],
        compiler_params=_params("parallel", "arbitrary"),
        name="sb_attention",
    )(sbq, sbk, sbvt, tneg)


def _mla_kernel(q_ref, k_ref, vt_ref, o_ref, m_sc, l_sc, acc_sc, s_sc, smax_sc, p_sc):
    tb = MLA_TILE
    sub = ATT_TILE
    i = pl.program_id(1)
    m_sc[...] = jnp.full(m_sc.shape, -1e30, _F32)
    l_sc[...] = jnp.zeros(l_sc.shape, _F32)
    acc_sc[...] = jnp.zeros(acc_sc.shape, _F32)
    key_row = lax.broadcasted_iota(jnp.int32, (tb, tb), 0)
    qry_col = lax.broadcasted_iota(jnp.int32, (tb, tb), 1)
    causal = key_row <= qry_col

    def scores(j, h):
        rows = pl.ds(pl.multiple_of(j * tb, tb), tb)
        kb = k_ref[rows, h * MLA_SLAB:(h + 1) * MLA_SLAB]
        q = q_ref[:, h * MLA_SLAB:(h + 1) * MLA_SLAB]
        return lax.dot_general(kb, q, _NT, preferred_element_type=_F32)

    for h in range(MLA_HEADS):
        s = scores(0, h)
        s_sc[h] = s
        smax_sc[h] = jnp.max(s, axis=0, keepdims=True)

    def all_heads(j, valid, prefetch_next):
        s_next = [scores(j + 1, h) for h in range(MLA_HEADS)] if prefetch_next else []
        new_m, new_l, new_acc = [], [], []
        for h in range(MLA_HEADS):
            m_old = m_sc[h]
            if valid is None:
                m_new = jnp.maximum(m_old, smax_sc[h])
                p = jnp.exp2(s_sc[h] - m_new)
            else:
                s = jnp.where(valid, s_sc[h], -jnp.inf)
                m_new = jnp.maximum(m_old, jnp.max(s, axis=0, keepdims=True))
                p = jnp.exp2(s - m_new)
            alpha = jnp.exp2(m_old - m_new)
            new_l.append(alpha * l_sc[h] + jnp.sum(p, axis=0, keepdims=True))
            new_m.append(m_new)
            p_sc[h] = p.astype(_BF)
            pv = None
            for c in range(tb // sub):
                vt = vt_ref[0, j * (tb // sub) + c, h * MLA_V_DIM:(h + 1) * MLA_V_DIM, :]
                part = jnp.dot(vt, p_sc[h, c * sub:(c + 1) * sub, :], preferred_element_type=_F32)
                pv = part if pv is None else pv + part
            new_acc.append(alpha * acc_sc[h] + pv)
        for h in range(MLA_HEADS):
            m_sc[h] = new_m[h]
            l_sc[h] = new_l[h]
            acc_sc[h] = new_acc[h]
        for h, s in enumerate(s_next):
            s_sc[h] = s
            smax_sc[h] = jnp.max(s, axis=0, keepdims=True)

    def body(j, c):
        all_heads(j, None, True)
        return c

    lax.fori_loop(0, i, body, 0)
    all_heads(i, causal, False)
    out_t = jnp.concatenate([acc_sc[h] / l_sc[h] for h in range(MLA_HEADS)], axis=0)
    o_ref[...] = out_t.T


def _mla_attention(mq, mk, mvt, batch, seq):
    tb = MLA_TILE
    nblk = seq // tb
    return pl.pallas_call(
        _mla_kernel,
        grid=(batch, nblk),
        in_specs=[
            pl.BlockSpec((tb, MLA_HEADS * MLA_SLAB), lambda b, i: (b * nblk + i, 0)),
            pl.BlockSpec((seq, MLA_HEADS * MLA_SLAB), lambda b, i: (b, 0),
                         pipeline_mode=pl.Buffered(1)),
            pl.BlockSpec((1,) + mvt.shape[1:], lambda b, i: (b, 0, 0, 0),
                         pipeline_mode=pl.Buffered(1)),
        ],
        out_specs=pl.BlockSpec((tb, MLA_WIDTH), lambda b, i: (b * nblk + i, 0)),
        out_shape=jax.ShapeDtypeStruct((batch * seq, MLA_WIDTH), _F32),
        scratch_shapes=[pltpu.VMEM((MLA_HEADS, 1, tb), _F32), pltpu.VMEM((MLA_HEADS, 1, tb), _F32),
                        pltpu.VMEM((MLA_HEADS, MLA_V_DIM, tb), _F32),
                        pltpu.VMEM((MLA_HEADS, tb, tb), _F32),
                        pltpu.VMEM((MLA_HEADS, 1, tb), _F32),
                        pltpu.VMEM((MLA_HEADS, tb, tb), _BF)],
        compiler_params=_params("parallel", "arbitrary"),
        name="mla_attention",
    )(mq, mk, mvt)


def _post_kernel(h_ref, sb_ref, mla_ref, p_ref, gsb_ref, gmla_ref, wout_ref,
                 gffn_ref, wg_ref, wu_ref, wd_ref, gple_ref, wpg_ref, wpp_ref, gfin_ref,
                 o_ref, *, final_norm):
    sb = _rms(sb_ref[...], gsb_ref[...]).astype(_BF)
    ml = _rms(mla_ref[...], gmla_ref[...]).astype(_BF)
    h1 = h_ref[...] + jnp.dot(sb, wout_ref[:SB_WIDTH, :], preferred_element_type=_F32)
    h1 = h1 + jnp.dot(ml, wout_ref[SB_WIDTH:, :], preferred_element_type=_F32)
    h2 = _swiglu_half_step(h1, gffn_ref, wg_ref, wu_ref, wd_ref)
    hn = _rms(h2, gple_ref[...]).astype(_BF)
    gate = jax.nn.sigmoid(jnp.dot(hn, wpg_ref[...], preferred_element_type=_F32))
    emb = jnp.dot(p_ref[...].astype(_BF), wpp_ref[...], preferred_element_type=_F32)
    out = h2 + gate * emb
    if final_norm:
        out = _rms(out, gfin_ref[...])
    o_ref[...] = out


def _post_mixer(h, sb, mla, p, li, gsb, gmla, wout, gffn, wg, wu, wd, gple, wpg, wpp, gfin,
                final_norm):
    t = h.shape[0]
    tm = POST_TILE
    row = lambda n: pl.BlockSpec((tm, n), lambda i: (i, 0))
    layered = (gsb, gmla, wout, gffn, wg, wu, wd, gple, wpg, wpp)
    return pl.pallas_call(
        functools.partial(_post_kernel, final_norm=final_norm),
        grid=(t // tm,),
        in_specs=[row(D_MODEL), row(SB_WIDTH), row(MLA_WIDTH),
                  pl.BlockSpec((None, tm, D_PLE), lambda i: (li, i, 0))]
                 + [_layer(a, li) for a in layered] + [_resident(gfin.shape)],
        out_specs=row(D_MODEL),
        out_shape=jax.ShapeDtypeStruct(h.shape, _F32),
        compiler_params=_params("parallel"),
        name="post_mixer",
    )(h, sb, mla, p, *layered, gfin)


def _swap_cols(w):
    half = w.shape[-1] // 2
    return jnp.concatenate([-w[..., half:], w[..., :half]], axis=-1)


def _mixer_weights(w_in, q_lat_norm, w_uq, kv_lat_norm, w_ukv):
    o1, o2, o3 = SB_WIDTH, 2 * SB_WIDTH, 3 * SB_WIDTH
    o4 = o3 + Q_LORA
    o5 = o4 + KV_LORA
    w_kr = w_in[:, o5:]
    wlat = jnp.concatenate([w_in[:, o3:o5], w_kr, _swap_cols(w_kr)], axis=1)
    uq = w_uq.reshape(Q_LORA, MLA_HEADS, MLA_QK_DIM)
    wuq = jnp.concatenate([uq, _swap_cols(uq[:, :, MLA_NOPE_DIM:])], axis=-1)
    ukv = w_ukv.reshape(KV_LORA, MLA_HEADS, MLA_NOPE_DIM + MLA_V_DIM)
    wuk = ukv[:, :, :MLA_NOPE_DIM].reshape(KV_LORA, -1)
    return {
        "wq": w_in[:, :o1].astype(_BF),
        "wk": w_in[:, o1:o2].astype(_BF),
        "wvt": w_in[:, o2:o3].T.astype(_BF),
        "wlat": wlat.astype(_BF),
        "gq": q_lat_norm.reshape(1, -1),
        "gkv": kv_lat_norm.reshape(1, -1),
        "wuq": wuq.reshape(Q_LORA, -1).astype(_BF),
        "wuk": wuk.astype(_BF),
        "wuvt": ukv[:, :, MLA_NOPE_DIM:].reshape(KV_LORA, -1).T.astype(_BF),
    }


def _rope_tables(seq):
    half = MLA_ROPE_DIM // 2
    inv_freq = ROPE_THETA ** (-jnp.arange(half, dtype=_F32) / half)
    ang = jnp.arange(seq, dtype=_F32)[:, None] * inv_freq[None, :]
    cos, sin = jnp.cos(ang), jnp.sin(ang)
    pad = jnp.zeros((seq, LANES - MLA_ROPE_DIM), _F32)
    return (jnp.concatenate([cos, cos, pad], axis=1), jnp.concatenate([sin, sin, pad], axis=1))


def kernel(x, p, ffn1_norm, ffn1_w_gate, ffn1_w_up, ffn1_w_down, mix_norm, w_in, q_lat_norm, w_uq, kv_lat_norm, w_ukv, sb_out_norm, mla_out_norm, w_out, ffn2_norm, ffn2_w_gate, ffn2_w_up, ffn2_w_down, ple_norm, w_ple_gate, w_ple_proj, final_norm):
    batch, seq, _ = x.shape
    depth = p.shape[0]
    assert seq % MLA_TILE == 0 and seq % ATT_TILE == 0 and (batch * seq) % TOK_TILE == 0
    t = batch * seq
    cos2, sin2 = _rope_tables(seq)
    idx = jnp.arange(ATT_TILE)
    tneg = jnp.where(idx[None, :] > idx[:, None], -1.0, 0.0).astype(_BF)
    vec = lambda v: v.reshape(1, -1)
    gains = lambda g: g.reshape(depth, 1, -1)
    p_tok = p.reshape(depth, t, D_PLE)
    w_out_bf, w_ple_gate_bf, w_ple_proj_bf = (w.astype(_BF) for w in (w_out, w_ple_gate, w_ple_proj))

    h = x.reshape(t, D_MODEL)
    for li in range(depth):
        h = _ffn(h, li, gains(ffn1_norm), ffn1_w_gate, ffn1_w_up, ffn1_w_down)
        mw = _mixer_weights(w_in[li], q_lat_norm[li], w_uq[li], kv_lat_norm[li], w_ukv[li])
        sbq, sbk, sbvt, mq, mk, mvt = _proj(h, vec(mix_norm[li]), mw, cos2, sin2, batch, seq)
        sb = _sb_attention(sbq, sbk, sbvt, tneg, batch, seq)
        mla = _mla_attention(mq, mk, mvt, batch, seq)
        h = _post_mixer(h, sb, mla, p_tok, li, gains(sb_out_norm), gains(mla_out_norm), w_out_bf,
                        gains(ffn2_norm), ffn2_w_gate, ffn2_w_up, ffn2_w_down,
                        gains(ple_norm), w_ple_gate_bf, w_ple_proj_bf, vec(final_norm),
                        li == depth - 1)
    return h.reshape(batch, seq, D_MODEL)
```

```python
import functools
import math

import jax
import jax.numpy as jnp
import numpy as np
from jax import lax
from jax.experimental import pallas as pl
from jax.experimental.pallas import tpu as pltpu

D_MODEL = 1024
D_PLE = 256
SB_HEADS = 8
SB_HEAD_DIM = 64
SB_WIDTH = SB_HEADS * SB_HEAD_DIM
MLA_HEADS = 4
MLA_NOPE_DIM = 128
MLA_ROPE_DIM = 64
MLA_QK_DIM = MLA_NOPE_DIM + MLA_ROPE_DIM
MLA_V_DIM = 128
MLA_WIDTH = MLA_HEADS * MLA_V_DIM
Q_LORA = 256
KV_LORA = 128
D_FF = 2816
ROPE_THETA = 10000.0
EPS = 1e-6

LANES = 128
MXU_DIM = 256
VMEM_LIMIT_BYTES = 58 * 1024 * 1024

FF_CHUNK = MXU_DIM
N_FF_CHUNKS = D_FF // FF_CHUNK
TOK_TILE = 512
POST_TILE = 512
ATT_TILE = 256
MLA_TILE = 512
MLA_SLAB = 2 * LANES
SB_DEAD_LOG2 = -152.0
SB_MASKED_LOGIT = -1e30

_BF = jnp.bfloat16
_F32 = jnp.float32
_NT = (((1,), (1,)), ((), ()))


def _rms(x, g):
    return x * lax.rsqrt(jnp.mean(x * x, axis=-1, keepdims=True) + EPS) * g


def _params(*sem):
    return pltpu.CompilerParams(dimension_semantics=sem, vmem_limit_bytes=VMEM_LIMIT_BYTES)


def _resident(shape):
    zeros = (0,) * len(shape)
    return pl.BlockSpec(shape, lambda *_: zeros, pipeline_mode=pl.Buffered(1))


def _layer(arr, li):
    index = (li,) + (0,) * (arr.ndim - 1)
    return pl.BlockSpec((None,) + arr.shape[1:], lambda *_: index, pipeline_mode=pl.Buffered(1))


def _swiglu_half_step(x, g_ref, wg_ref, wu_ref, wd_ref):
    xn = _rms(x, g_ref[...]).astype(_BF)
    acc = jnp.zeros(x.shape, _F32)
    for c in range(N_FF_CHUNKS):
        cols = slice(c * FF_CHUNK, (c + 1) * FF_CHUNK)
        gate = jnp.dot(xn, wg_ref[:, cols].astype(_BF), preferred_element_type=_F32)
        up = jnp.dot(xn, wu_ref[:, cols].astype(_BF), preferred_element_type=_F32)
        hid = (gate * jax.nn.sigmoid(gate) * up).astype(_BF)
        acc = acc + jnp.dot(hid, wd_ref[cols, :].astype(_BF), preferred_element_type=_F32)
    return x + 0.5 * acc


def _ffn_kernel(x_ref, g_ref, wg_ref, wu_ref, wd_ref, o_ref):
    o_ref[...] = _swiglu_half_step(x_ref[...], g_ref, wg_ref, wu_ref, wd_ref)


def _ffn(x, li, g, wg, wu, wd):
    t = x.shape[0]
    row = pl.BlockSpec((TOK_TILE, D_MODEL), lambda i: (i, 0))
    return pl.pallas_call(
        _ffn_kernel,
        grid=(t // TOK_TILE,),
        in_specs=[row, _layer(g, li), _layer(wg, li), _layer(wu, li), _layer(wd, li)],
        out_specs=row,
        out_shape=jax.ShapeDtypeStruct(x.shape, _F32),
        compiler_params=_params("parallel"),
        name="ffn",
    )(x, g, wg, wu, wd)


def _rope_slab(x, cos, sin):
    return x * cos + pltpu.roll(x, LANES // 2, axis=1) * sin


def _proj_kernel(h_ref, g_ref, wq_ref, wk_ref, wvt_ref, wlat_ref, gq_ref, gkv_ref,
                 wuq_ref, wuk_ref, wuvt_ref, cos_ref, sin_ref,
                 sbq_ref, sbk_ref, sbvt_ref, mq_ref, mk_ref, mvt_ref):
    u = _rms(h_ref[...], g_ref[...]).astype(_BF)
    sb_scale = SB_HEAD_DIM ** -0.5 * math.log2(math.e)
    sbq_ref[...] = (jnp.dot(u, wq_ref[...], preferred_element_type=_F32) * sb_scale).astype(_BF)
    sbk_ref[...] = jnp.dot(u, wk_ref[...], preferred_element_type=_F32).astype(_BF)
    sbvt = lax.dot_general(wvt_ref[...], u, _NT, preferred_element_type=_F32).astype(_BF)
    for c in range(sbvt_ref.shape[1]):
        sbvt_ref[0, c] = sbvt[:, c * ATT_TILE:(c + 1) * ATT_TILE]

    lat = jnp.dot(u, wlat_ref[...], preferred_element_type=_F32)
    cos = cos_ref[...]
    sin = sin_ref[...]
    cq = _rms(lat[:, :Q_LORA], gq_ref[...]).astype(_BF)
    ckv = _rms(lat[:, Q_LORA:Q_LORA + KV_LORA], gkv_ref[...]).astype(_BF)
    o_kr = Q_LORA + KV_LORA
    k_rope = _rope_slab(lat[:, o_kr:o_kr + LANES], cos, sin).astype(_BF)

    qa = jnp.dot(cq, wuq_ref[...], preferred_element_type=_F32)
    kn = jnp.dot(ckv, wuk_ref[...], preferred_element_type=_F32)
    mvt = lax.dot_general(wuvt_ref[...], ckv, _NT, preferred_element_type=_F32).astype(_BF)
    for c in range(mvt_ref.shape[1]):
        mvt_ref[0, c] = mvt[:, c * ATT_TILE:(c + 1) * ATT_TILE]
    scale = MLA_QK_DIM ** -0.5 * math.log2(math.e)
    for hd in range(MLA_HEADS):
        o = hd * MLA_SLAB
        mq_ref[:, o:o + LANES] = (qa[:, o:o + LANES] * scale).astype(_BF)
        roped = _rope_slab(qa[:, o + LANES:o + 2 * LANES], cos, sin)
        mq_ref[:, o + LANES:o + 2 * LANES] = (roped * scale).astype(_BF)
        mk_ref[:, o:o + LANES] = kn[:, hd * LANES:(hd + 1) * LANES].astype(_BF)
        mk_ref[:, o + LANES:o + 2 * LANES] = k_rope


def _proj(h, li, g, w, cos2, sin2, batch, seq):
    t = batch * seq
    tm = TOK_TILE
    nblk = seq // tm
    kb = ATT_TILE
    row = lambda n: pl.BlockSpec((tm, n), lambda i: (i, 0))
    outs = (
        jax.ShapeDtypeStruct((t, SB_WIDTH), _BF),
        jax.ShapeDtypeStruct((t, SB_WIDTH), _BF),
        jax.ShapeDtypeStruct((batch, seq // kb, SB_WIDTH, kb), _BF),
        jax.ShapeDtypeStruct((t, MLA_HEADS * MLA_SLAB), _BF),
        jax.ShapeDtypeStruct((t, MLA_HEADS * MLA_SLAB), _BF),
        jax.ShapeDtypeStruct((batch, seq // kb, MLA_WIDTH, kb), _BF),
    )
    weights = (g, w["wq"], w["wk"], w["wvt"], w["wlat"], w["gq"], w["gkv"],
               w["wuq"], w["wuk"], w["wuvt"])
    vt_spec = pl.BlockSpec((1, tm // kb, SB_WIDTH, kb), lambda i: (i // nblk, i % nblk, 0, 0))
    rope_spec = pl.BlockSpec((tm, LANES), lambda i: (i % nblk, 0))
    return pl.pallas_call(
        _proj_kernel,
        grid=(t // tm,),
        in_specs=[row(D_MODEL)] + [_layer(a, li) for a in weights] + [rope_spec, rope_spec],
        out_specs=(row(SB_WIDTH), row(SB_WIDTH), vt_spec,
                   row(MLA_HEADS * MLA_SLAB), row(MLA_HEADS * MLA_SLAB), vt_spec),
        out_shape=outs,
        compiler_params=_params("parallel"),
        name="mixer_proj",
    )(h, *weights, cos2, sin2)


def _sb_kernel(q_ref, k_ref, vt_ref, tneg_ref, o_ref, carry_sc, acc_sc, spb_sc, lsig_sc, ab_sc):
    tb = ATT_TILE
    hd = SB_HEAD_DIM
    i = pl.program_id(1)
    lane = lax.broadcasted_iota(jnp.int32, (tb, LANES), 1)
    tneg = tneg_ref[...]
    key_row = lax.broadcasted_iota(jnp.int32, (tb, SB_HEADS * tb), 0)
    qry_col = lax.broadcasted_iota(jnp.int32, (tb, SB_HEADS * tb), 1) & (tb - 1)
    valid = key_row < qry_col

    carry_sc[...] = jnp.zeros(carry_sc.shape, _F32)
    acc_sc[...] = jnp.zeros(acc_sc.shape, _F32)

    q_pairs = []
    for pair in range(SB_HEADS // 2):
        q = q_ref[:, pair * LANES:(pair + 1) * LANES]
        zero = jnp.zeros_like(q)
        q_pairs.append(jnp.concatenate([jnp.where(lane < hd, q, zero),
                                        jnp.where(lane >= hd, q, zero)], axis=0))

    def key_blocks(blocks):
        operands = []
        for j, _ in blocks:
            rows = pl.ds(pl.multiple_of(j * tb, tb), tb)
            operands.append((
                [k_ref[rows, pair * LANES:(pair + 1) * LANES] for pair in range(SB_HEADS // 2)],
                [vt_ref[0, j, h * hd:(h + 1) * hd, :] for h in range(SB_HEADS)]))
        carry = carry_sc[...]
        total = None
        for slot, ((_, mask), (kbs, vbs)) in enumerate(zip(blocks, operands)):
            z = jnp.concatenate([lax.dot_general(kb, qp, _NT, preferred_element_type=_F32)
                                 for kb, qp in zip(kbs, q_pairs)], axis=1)
            if mask is not None:
                z = jnp.where(mask, z, SB_MASKED_LOGIT)
            sp = jnp.maximum(z, 0.0) + jnp.log2(1.0 + jnp.exp2(-jnp.abs(z)))
            spb_sc[slot] = sp.astype(_BF)
            lsig_sc[slot] = z - sp
            tail = jnp.dot(tneg, spb_sc[slot], preferred_element_type=_F32)
            ab_sc[slot] = jnp.exp2(lsig_sc[slot] + tail + carry).astype(_BF)
            contrib = jnp.concatenate(
                [jnp.dot(vbs[h], ab_sc[slot, :, h * tb:(h + 1) * tb], preferred_element_type=_F32)
                 for h in range(SB_HEADS)], axis=0)
            carry = carry + tail[0:1, :] - sp[0:1, :]
            total = contrib if total is None else total + contrib
        carry_sc[...] = carry
        acc_sc[...] += total

    def alive():
        return (jnp.max(carry_sc[...]) > SB_DEAD_LOG2).astype(jnp.int32)

    @pl.when(i == 0)
    def _():
        key_blocks([(i, valid)])

    @pl.when(i > 0)
    def _():
        key_blocks([(i, valid), (i - 1, None)])

    def cond(state):
        j, live = state
        return jnp.logical_and(j >= 0, live > 0)

    def body(state):
        j, _ = state
        key_blocks([(j, None)])
        return j - 1, alive()

    lax.while_loop(cond, body, (i - 2, alive()))
    o_ref[...] = acc_sc[...].T


def _sb_attention(sbq, sbk, sbvt, tneg, batch, seq):
    tb = ATT_TILE
    nblk = seq // tb
    return pl.pallas_call(
        _sb_kernel,
        grid=(batch, nblk),
        in_specs=[
            pl.BlockSpec((tb, SB_WIDTH), lambda b, i: (b * nblk + i, 0)),
            pl.BlockSpec((seq, SB_WIDTH), lambda b, i: (b, 0), pipeline_mode=pl.Buffered(1)),
            pl.BlockSpec((1, nblk, SB_WIDTH, tb), lambda b, i: (b, 0, 0, 0),
                         pipeline_mode=pl.Buffered(1)),
            _resident(tneg.shape),
        ],
        out_specs=pl.BlockSpec((tb, SB_WIDTH), lambda b, i: (b * nblk + i, 0)),
        out_shape=jax.ShapeDtypeStruct((batch * seq, SB_WIDTH), _F32),
        scratch_shapes=[pltpu.VMEM((1, SB_HEADS * tb), _F32), pltpu.VMEM((SB_WIDTH, tb), _F32),
                        pltpu.VMEM((2, tb, SB_HEADS * tb), _BF),
                        pltpu.VMEM((2, tb, SB_HEADS * tb), _F32),
                        pltpu.VMEM((2, tb, SB_HEADS * tb), _BF)],
        compiler_params=_params("parallel", "arbitrary"),
        name="sb_attention",
    )(sbq, sbk, sbvt, tneg)


def _mla_kernel(q_ref, k_ref, vt_ref, o_ref, m_sc, l_sc, acc_sc, s_sc, smax_sc, p_sc):
    tb = MLA_TILE
    sub = ATT_TILE
    i = pl.program_id(1)
    m_sc[...] = jnp.full(m_sc.shape, -1e30, _F32)
    l_sc[...] = jnp.zeros(l_sc.shape, _F32)
    acc_sc[...] = jnp.zeros(acc_sc.shape, _F32)
    key_row = lax.broadcasted_iota(jnp.int32, (tb, tb), 0)
    qry_col = lax.broadcasted_iota(jnp.int32, (tb, tb), 1)
    causal = key_row <= qry_col

    def scores(j, h):
        rows = pl.ds(pl.multiple_of(j * tb, tb), tb)
        kb = k_ref[rows, h * MLA_SLAB:(h + 1) * MLA_SLAB]
        q = q_ref[:, h * MLA_SLAB:(h + 1) * MLA_SLAB]
        return lax.dot_general(kb, q, _NT, preferred_element_type=_F32)

    for h in range(MLA_HEADS):
        s = scores(0, h)
        s_sc[h] = s
        smax_sc[h] = jnp.max(s, axis=0, keepdims=True)

    def all_heads(j, valid, prefetch_next):
        s_next = [scores(j + 1, h) for h in range(MLA_HEADS)] if prefetch_next else []
        new_m, new_l, new_acc = [], [], []
        for h in range(MLA_HEADS):
            m_old = m_sc[h]
            if valid is None:
                m_new = jnp.maximum(m_old, smax_sc[h])
                p = jnp.exp2(s_sc[h] - m_new)
            else:
                s = jnp.where(valid, s_sc[h], -jnp.inf)
                m_new = jnp.maximum(m_old, jnp.max(s, axis=0, keepdims=True))
                p = jnp.exp2(s - m_new)
            alpha = jnp.exp2(m_old - m_new)
            new_l.append(alpha * l_sc[h] + jnp.sum(p, axis=0, keepdims=True))
            new_m.append(m_new)
            p_sc[h] = p.astype(_BF)
            pv = None
            for c in range(tb // sub):
                vt = vt_ref[0, j * (tb // sub) + c, h * MLA_V_DIM:(h + 1) * MLA_V_DIM, :]
                part = jnp.dot(vt, p_sc[h, c * sub:(c + 1) * sub, :], preferred_element_type=_F32)
                pv = part if pv is None else pv + part
            new_acc.append(alpha * acc_sc[h] + pv)
        for h in range(MLA_HEADS):
            m_sc[h] = new_m[h]
            l_sc[h] = new_l[h]
            acc_sc[h] = new_acc[h]
        for h, s in enumerate(s_next):
            s_sc[h] = s
            smax_sc[h] = jnp.max(s, axis=0, keepdims=True)

    def body(j, c):
        all_heads(j, None, True)
        return c

    lax.fori_loop(0, i, body, 0)
    all_heads(i, causal, False)
    out_t = jnp.concatenate([acc_sc[h] / l_sc[h] for h in range(MLA_HEADS)], axis=0)
    o_ref[...] = out_t.T


def _mla_attention(mq, mk, mvt, batch, seq):
    tb = MLA_TILE
    nblk = seq // tb
    return pl.pallas_call(
        _mla_kernel,
        grid=(batch, nblk),
        in_specs=[
            pl.BlockSpec((tb, MLA_HEADS * MLA_SLAB), lambda b, i: (b * nblk + i, 0)),
            pl.BlockSpec((seq, MLA_HEADS * MLA_SLAB), lambda b, i: (b, 0),
                         pipeline_mode=pl.Buffered(1)),
            pl.BlockSpec((1,) + mvt.shape[1:], lambda b, i: (b, 0, 0, 0),
                         pipeline_mode=pl.Buffered(1)),
        ],
        out_specs=pl.BlockSpec((tb, MLA_WIDTH), lambda b, i: (b * nblk + i, 0)),
        out_shape=jax.ShapeDtypeStruct((batch * seq, MLA_WIDTH), _F32),
        scratch_shapes=[pltpu.VMEM((MLA_HEADS, 1, tb), _F32), pltpu.VMEM((MLA_HEADS, 1, tb), _F32),
                        pltpu.VMEM((MLA_HEADS, MLA_V_DIM, tb), _F32),
                        pltpu.VMEM((MLA_HEADS, tb, tb), _F32),
                        pltpu.VMEM((MLA_HEADS, 1, tb), _F32),
                        pltpu.VMEM((MLA_HEADS, tb, tb), _BF)],
        compiler_params=_params("parallel", "arbitrary"),
        name="mla_attention",
    )(mq, mk, mvt)


def _post_kernel(h_ref, sb_ref, mla_ref, p_ref, gsb_ref, gmla_ref, wout_ref,
                 gffn_ref, wg_ref, wu_ref, wd_ref, gple_ref, wpg_ref, wpp_ref, gfin_ref,
                 o_ref, *, final_norm):
    sb = _rms(sb_ref[...], gsb_ref[...]).astype(_BF)
    ml = _rms(mla_ref[...], gmla_ref[...]).astype(_BF)
    h1 = h_ref[...] + jnp.dot(sb, wout_ref[:SB_WIDTH, :], preferred_element_type=_F32)
    h1 = h1 + jnp.dot(ml, wout_ref[SB_WIDTH:, :], preferred_element_type=_F32)
    h2 = _swiglu_half_step(h1, gffn_ref, wg_ref, wu_ref, wd_ref)
    hn = _rms(h2, gple_ref[...]).astype(_BF)
    gate = jax.nn.sigmoid(jnp.dot(hn, wpg_ref[...], preferred_element_type=_F32))
    emb = jnp.dot(p_ref[...].astype(_BF), wpp_ref[...], preferred_element_type=_F32)
    out = h2 + gate * emb
    if final_norm:
        out = _rms(out, gfin_ref[...])
    o_ref[...] = out


def _post_mixer(h, sb, mla, p, li, gsb, gmla, wout, gffn, wg, wu, wd, gple, wpg, wpp, gfin,
                final_norm):
    t = h.shape[0]
    tm = POST_TILE
    row = lambda n: pl.BlockSpec((tm, n), lambda i: (i, 0))
    layered = (gsb, gmla, wout, gffn, wg, wu, wd, gple, wpg, wpp)
    return pl.pallas_call(
        functools.partial(_post_kernel, final_norm=final_norm),
        grid=(t // tm,),
        in_specs=[row(D_MODEL), row(SB_WIDTH), row(MLA_WIDTH),
                  pl.BlockSpec((None, tm, D_PLE), lambda i: (li, i, 0))]
                 + [_layer(a, li) for a in layered] + [_resident(gfin.shape)],
        out_specs=row(D_MODEL),
        out_shape=jax.ShapeDtypeStruct(h.shape, _F32),
        compiler_params=_params("parallel"),
        name="post_mixer",
    )(h, sb, mla, p, *layered, gfin)


def _swap_cols(w):
    half = w.shape[-1] // 2
    return jnp.concatenate([-w[..., half:], w[..., :half]], axis=-1)


def _mixer_weights(w_in, q_lat_norm, w_uq, kv_lat_norm, w_ukv):
    depth = w_in.shape[0]
    o1, o2, o3 = SB_WIDTH, 2 * SB_WIDTH, 3 * SB_WIDTH
    o4 = o3 + Q_LORA
    o5 = o4 + KV_LORA
    w_kr = w_in[..., o5:]
    wlat = jnp.concatenate([w_in[..., o3:o5], w_kr, _swap_cols(w_kr)], axis=-1)
    uq = w_uq.reshape(depth, Q_LORA, MLA_HEADS, MLA_QK_DIM)
    wuq = jnp.concatenate([uq, _swap_cols(uq[..., MLA_NOPE_DIM:])], axis=-1)
    ukv = w_ukv.reshape(depth, KV_LORA, MLA_HEADS, MLA_NOPE_DIM + MLA_V_DIM)
    transposed = lambda w: jnp.swapaxes(w, -1, -2).astype(_BF)
    return {
        "wq": w_in[..., :o1].astype(_BF),
        "wk": w_in[..., o1:o2].astype(_BF),
        "wvt": transposed(w_in[..., o2:o3]),
        "wlat": wlat.astype(_BF),
        "gq": q_lat_norm.reshape(depth, 1, -1),
        "gkv": kv_lat_norm.reshape(depth, 1, -1),
        "wuq": wuq.reshape(depth, Q_LORA, -1).astype(_BF),
        "wuk": ukv[..., :MLA_NOPE_DIM].reshape(depth, KV_LORA, -1).astype(_BF),
        "wuvt": transposed(ukv[..., MLA_NOPE_DIM:].reshape(depth, KV_LORA, -1)),
    }


def _rope_tables(seq):
    half = MLA_ROPE_DIM // 2
    inv_freq = ROPE_THETA ** (-np.arange(half, dtype=np.float64) / half)
    ang = np.arange(seq, dtype=np.float64)[:, None] * inv_freq[None, :]
    pad = np.zeros((seq, LANES - MLA_ROPE_DIM))
    table = lambda f: jnp.asarray(np.concatenate([f(ang), f(ang), pad], axis=1), _F32)
    return table(np.cos), table(np.sin)


def kernel(x, p, ffn1_norm, ffn1_w_gate, ffn1_w_up, ffn1_w_down, mix_norm, w_in, q_lat_norm, w_uq, kv_lat_norm, w_ukv, sb_out_norm, mla_out_norm, w_out, ffn2_norm, ffn2_w_gate, ffn2_w_up, ffn2_w_down, ple_norm, w_ple_gate, w_ple_proj, final_norm):
    batch, seq, _ = x.shape
    depth = p.shape[0]
    assert seq % MLA_TILE == 0 and seq % ATT_TILE == 0 and (batch * seq) % TOK_TILE == 0
    t = batch * seq
    cos2, sin2 = _rope_tables(seq)
    idx = np.arange(ATT_TILE)
    tneg = jnp.asarray(np.where(idx[None, :] > idx[:, None], -1.0, 0.0), _BF)
    vec = lambda v: v.reshape(1, -1)
    gains = lambda g: g.reshape(depth, 1, -1)
    p_tok = p.reshape(depth, t, D_PLE)
    w_out_bf, w_ple_gate_bf, w_ple_proj_bf = (w.astype(_BF) for w in (w_out, w_ple_gate, w_ple_proj))
    mw = _mixer_weights(w_in, q_lat_norm, w_uq, kv_lat_norm, w_ukv)

    h = x.reshape(t, D_MODEL)
    for li in range(depth):
        h = _ffn(h, li, gains(ffn1_norm), ffn1_w_gate, ffn1_w_up, ffn1_w_down)
        sbq, sbk, sbvt, mq, mk, mvt = _proj(h, li, gains(mix_norm), mw, cos2, sin2, batch, seq)
        sb = _sb_attention(sbq, sbk, sbvt, tneg, batch, seq)
        mla = _mla_attention(mq, mk, mvt, batch, seq)
        h = _post_mixer(h, sb, mla, p_tok, li, gains(sb_out_norm), gains(mla_out_norm), w_out_bf,
                        gains(ffn2_norm), ffn2_w_gate, ffn2_w_up, ffn2_w_down,
                        gains(ple_norm), w_ple_gate_bf, w_ple_proj_bf, vec(final_norm),
                        li == depth - 1)
    return h.reshape(batch, seq, D_MODEL)
```

```python
import functools
import math

import jax
import jax.numpy as jnp
import numpy as np
from jax import lax
from jax.experimental import pallas as pl
from jax.experimental.pallas import tpu as pltpu

D_MODEL = 1024
D_PLE = 256
SB_HEADS = 8
SB_HEAD_DIM = 64
SB_WIDTH = SB_HEADS * SB_HEAD_DIM
MLA_HEADS = 4
MLA_NOPE_DIM = 128
MLA_ROPE_DIM = 64
MLA_QK_DIM = MLA_NOPE_DIM + MLA_ROPE_DIM
MLA_V_DIM = 128
MLA_WIDTH = MLA_HEADS * MLA_V_DIM
Q_LORA = 256
KV_LORA = 128
D_FF = 2816
ROPE_THETA = 10000.0
EPS = 1e-6

LANES = 128
MXU_DIM = 256
VMEM_LIMIT_BYTES = 58 * 1024 * 1024

FF_CHUNK = MXU_DIM
N_FF_CHUNKS = D_FF // FF_CHUNK
TOK_TILE = 512
POST_TILE = 512
ATT_TILE = 256
MLA_TILE = 512
MLA_SLAB = 2 * LANES
SB_DEAD_LOG2 = -152.0
SB_MASKED_LOGIT = -1e30

_BF = jnp.bfloat16
_F32 = jnp.float32
_NT = (((1,), (1,)), ((), ()))


def _rms(x, g):
    return x * lax.rsqrt(jnp.mean(x * x, axis=-1, keepdims=True) + EPS) * g


def _params(*sem):
    return pltpu.CompilerParams(dimension_semantics=sem, vmem_limit_bytes=VMEM_LIMIT_BYTES)


def _resident(shape):
    zeros = (0,) * len(shape)
    return pl.BlockSpec(shape, lambda *_: zeros, pipeline_mode=pl.Buffered(1))


def _layer(arr, li):
    index = (li,) + (0,) * (arr.ndim - 1)
    return pl.BlockSpec((None,) + arr.shape[1:], lambda *_: index, pipeline_mode=pl.Buffered(1))


def _swiglu_half_step(x, g_ref, wg_ref, wu_ref, wd_ref):
    xn = _rms(x, g_ref[...]).astype(_BF)
    acc = jnp.zeros(x.shape, _F32)
    for c in range(N_FF_CHUNKS):
        cols = slice(c * FF_CHUNK, (c + 1) * FF_CHUNK)
        gate = jnp.dot(xn, wg_ref[:, cols].astype(_BF), preferred_element_type=_F32)
        up = jnp.dot(xn, wu_ref[:, cols].astype(_BF), preferred_element_type=_F32)
        hid = (gate * jax.nn.sigmoid(gate) * up).astype(_BF)
        acc = acc + jnp.dot(hid, wd_ref[cols, :].astype(_BF), preferred_element_type=_F32)
    return x + 0.5 * acc


def _ffn_kernel(x_ref, g_ref, wg_ref, wu_ref, wd_ref, o_ref):
    o_ref[...] = _swiglu_half_step(x_ref[...], g_ref, wg_ref, wu_ref, wd_ref)


def _ffn(x, li, g, wg, wu, wd):
    t = x.shape[0]
    row = pl.BlockSpec((TOK_TILE, D_MODEL), lambda i: (i, 0))
    return pl.pallas_call(
        _ffn_kernel,
        grid=(t // TOK_TILE,),
        in_specs=[row, _layer(g, li), _layer(wg, li), _layer(wu, li), _layer(wd, li)],
        out_specs=row,
        out_shape=jax.ShapeDtypeStruct(x.shape, _F32),
        compiler_params=_params("parallel"),
        name="ffn",
    )(x, g, wg, wu, wd)


def _rope_slab(x, cos, sin):
    return x * cos + pltpu.roll(x, LANES // 2, axis=1) * sin


def _proj_kernel(h_ref, g_ref, wq_ref, wk_ref, wvt_ref, wlat_ref, gq_ref, gkv_ref,
                 wuq_ref, wuk_ref, wuvt_ref, cos_ref, sin_ref,
                 sbq_ref, sbk_ref, sbvt_ref, mq_ref, mk_ref, mvt_ref):
    u = _rms(h_ref[...], g_ref[...]).astype(_BF)
    sb_scale = SB_HEAD_DIM ** -0.5 * math.log2(math.e)
    sbq_ref[...] = (jnp.dot(u, wq_ref[...], preferred_element_type=_F32) * sb_scale).astype(_BF)
    sbk_ref[...] = jnp.dot(u, wk_ref[...], preferred_element_type=_F32).astype(_BF)
    sbvt = lax.dot_general(wvt_ref[...], u, _NT, preferred_element_type=_F32).astype(_BF)
    for c in range(sbvt_ref.shape[1]):
        sbvt_ref[0, c] = sbvt[:, c * ATT_TILE:(c + 1) * ATT_TILE]

    lat = jnp.dot(u, wlat_ref[...], preferred_element_type=_F32)
    cos = cos_ref[...]
    sin = sin_ref[...]
    cq = _rms(lat[:, :Q_LORA], gq_ref[...]).astype(_BF)
    ckv = _rms(lat[:, Q_LORA:Q_LORA + KV_LORA], gkv_ref[...]).astype(_BF)
    o_kr = Q_LORA + KV_LORA
    k_rope = _rope_slab(lat[:, o_kr:o_kr + LANES], cos, sin).astype(_BF)

    qa = jnp.dot(cq, wuq_ref[...], preferred_element_type=_F32)
    kn = jnp.dot(ckv, wuk_ref[...], preferred_element_type=_F32)
    mvt = lax.dot_general(wuvt_ref[...], ckv, _NT, preferred_element_type=_F32).astype(_BF)
    for c in range(mvt_ref.shape[1]):
        mvt_ref[0, c] = mvt[:, c * ATT_TILE:(c + 1) * ATT_TILE]
    scale = MLA_QK_DIM ** -0.5 * math.log2(math.e)
    for hd in range(MLA_HEADS):
        o = hd * MLA_SLAB
        mq_ref[:, o:o + LANES] = (qa[:, o:o + LANES] * scale).astype(_BF)
        roped = _rope_slab(qa[:, o + LANES:o + 2 * LANES], cos, sin)
        mq_ref[:, o + LANES:o + 2 * LANES] = (roped * scale).astype(_BF)
        mk_ref[:, o:o + LANES] = kn[:, hd * LANES:(hd + 1) * LANES].astype(_BF)
        mk_ref[:, o + LANES:o + 2 * LANES] = k_rope


def _proj(h, li, g, w, cos2, sin2, batch, seq):
    t = batch * seq
    tm = TOK_TILE
    nblk = seq // tm
    kb = ATT_TILE
    row = lambda n: pl.BlockSpec((tm, n), lambda i: (i, 0))
    outs = (
        jax.ShapeDtypeStruct((t, SB_WIDTH), _BF),
        jax.ShapeDtypeStruct((t, SB_WIDTH), _BF),
        jax.ShapeDtypeStruct((batch, seq // kb, SB_WIDTH, kb), _BF),
        jax.ShapeDtypeStruct((t, MLA_HEADS * MLA_SLAB), _BF),
        jax.ShapeDtypeStruct((t, MLA_HEADS * MLA_SLAB), _BF),
        jax.ShapeDtypeStruct((batch, seq // kb, MLA_WIDTH, kb), _BF),
    )
    weights = (g, w["wq"], w["wk"], w["wvt"], w["wlat"], w["gq"], w["gkv"],
               w["wuq"], w["wuk"], w["wuvt"])
    vt_spec = pl.BlockSpec((1, tm // kb, SB_WIDTH, kb), lambda i: (i // nblk, i % nblk, 0, 0))
    rope_spec = pl.BlockSpec((tm, LANES), lambda i: (i % nblk, 0))
    return pl.pallas_call(
        _proj_kernel,
        grid=(t // tm,),
        in_specs=[row(D_MODEL)] + [_layer(a, li) for a in weights] + [rope_spec, rope_spec],
        out_specs=(row(SB_WIDTH), row(SB_WIDTH), vt_spec,
                   row(MLA_HEADS * MLA_SLAB), row(MLA_HEADS * MLA_SLAB), vt_spec),
        out_shape=outs,
        compiler_params=_params("parallel"),
        name="mixer_proj",
    )(h, *weights, cos2, sin2)


def _sb_kernel(q_ref, k_ref, vt_ref, tneg_ref, o_ref, carry_sc, acc_sc, spb_sc, lsig_sc, ab_sc):
    tb = ATT_TILE
    hd = SB_HEAD_DIM
    i = pl.program_id(1)
    lane = lax.broadcasted_iota(jnp.int32, (tb, LANES), 1)
    tneg = tneg_ref[...]
    key_row = lax.broadcasted_iota(jnp.int32, (tb, SB_HEADS * tb), 0)
    qry_col = lax.broadcasted_iota(jnp.int32, (tb, SB_HEADS * tb), 1) & (tb - 1)
    valid = key_row < qry_col

    carry_sc[...] = jnp.zeros(carry_sc.shape, _F32)
    acc_sc[...] = jnp.zeros(acc_sc.shape, _F32)

    q_pairs = []
    for pair in range(SB_HEADS // 2):
        q = q_ref[:, pair * LANES:(pair + 1) * LANES]
        zero = jnp.zeros_like(q)
        q_pairs.append(jnp.concatenate([jnp.where(lane < hd, q, zero),
                                        jnp.where(lane >= hd, q, zero)], axis=0))

    def key_blocks(blocks):
        operands = []
        for j, _ in blocks:
            rows = pl.ds(pl.multiple_of(j * tb, tb), tb)
            operands.append((
                [k_ref[rows, pair * LANES:(pair + 1) * LANES] for pair in range(SB_HEADS // 2)],
                [vt_ref[0, j, h * hd:(h + 1) * hd, :] for h in range(SB_HEADS)]))
        carry = carry_sc[...]
        total = None
        for slot, ((_, mask), (kbs, vbs)) in enumerate(zip(blocks, operands)):
            z = jnp.concatenate([lax.dot_general(kb, qp, _NT, preferred_element_type=_F32)
                                 for kb, qp in zip(kbs, q_pairs)], axis=1)
            if mask is not None:
                z = jnp.where(mask, z, SB_MASKED_LOGIT)
            sp = jnp.maximum(z, 0.0) + jnp.log2(1.0 + jnp.exp2(-jnp.abs(z)))
            spb_sc[slot] = sp.astype(_BF)
            lsig_sc[slot] = z - sp
            tail = jnp.dot(tneg, spb_sc[slot], preferred_element_type=_F32)
            ab_sc[slot] = jnp.exp2(lsig_sc[slot] + tail + carry).astype(_BF)
            contrib = jnp.concatenate(
                [jnp.dot(vbs[h], ab_sc[slot, :, h * tb:(h + 1) * tb], preferred_element_type=_F32)
                 for h in range(SB_HEADS)], axis=0)
            carry = carry + tail[0:1, :] - sp[0:1, :]
            total = contrib if total is None else total + contrib
        carry_sc[...] = carry
        acc_sc[...] += total

    def alive():
        return (jnp.max(carry_sc[...]) > SB_DEAD_LOG2).astype(jnp.int32)

    @pl.when(i == 0)
    def _():
        key_blocks([(i, valid)])

    @pl.when(i > 0)
    def _():
        key_blocks([(i, valid), (i - 1, None)])

    def cond(state):
        j, live = state
        return jnp.logical_and(j >= 0, live > 0)

    def body(state):
        j, _ = state
        key_blocks([(j, None)])
        return j - 1, alive()

    lax.while_loop(cond, body, (i - 2, alive()))
    o_ref[...] = acc_sc[...].T


def _sb_attention(sbq, sbk, sbvt, tneg, batch, seq):
    tb = ATT_TILE
    nblk = seq // tb
    return pl.pallas_call(
        _sb_kernel,
        grid=(batch, nblk),
        in_specs=[
            pl.BlockSpec((tb, SB_WIDTH), lambda b, i: (b * nblk + i, 0)),
            pl.BlockSpec((seq, SB_WIDTH), lambda b, i: (b, 0), pipeline_mode=pl.Buffered(1)),
            pl.BlockSpec((1, nblk, SB_WIDTH, tb), lambda b, i: (b, 0, 0, 0),
                         pipeline_mode=pl.Buffered(1)),
            _resident(tneg.shape),
        ],
        out_specs=pl.BlockSpec((tb, SB_WIDTH), lambda b, i: (b * nblk + i, 0)),
        out_shape=jax.ShapeDtypeStruct((batch * seq, SB_WIDTH), _F32),
        scratch_shapes=[pltpu.VMEM((1, SB_HEADS * tb), _F32), pltpu.VMEM((SB_WIDTH, tb), _F32),
                        pltpu.VMEM((2, tb, SB_HEADS * tb), _BF),
                        pltpu.VMEM((2, tb, SB_HEADS * tb), _F32),
                        pltpu.VMEM((2, tb, SB_HEADS * tb), _BF)],
        compiler_params=_params("parallel", "arbitrary"),
        name="sb_attention",
    )(sbq, sbk, sbvt, tneg)


def _mla_kernel(q_ref, qnext_ref, k_ref, vt_ref, o_ref, m_sc, l_sc, acc_sc, s_sc, smax_sc, p_sc):
    tb = MLA_TILE
    sub = ATT_TILE
    i = pl.program_id(1)
    m_sc[...] = jnp.full(m_sc.shape, -1e30, _F32)
    l_sc[...] = jnp.zeros(l_sc.shape, _F32)
    acc_sc[...] = jnp.zeros(acc_sc.shape, _F32)
    key_row = lax.broadcasted_iota(jnp.int32, (tb, tb), 0)
    qry_col = lax.broadcasted_iota(jnp.int32, (tb, tb), 1)
    causal = key_row <= qry_col

    def scores(j, h):
        rows = pl.ds(pl.multiple_of(j * tb, tb), tb)
        kb = k_ref[rows, h * MLA_SLAB:(h + 1) * MLA_SLAB]
        q = q_ref[:, h * MLA_SLAB:(h + 1) * MLA_SLAB]
        return lax.dot_general(kb, q, _NT, preferred_element_type=_F32)

    def first_scores(query_ref, h):
        kb = k_ref[0:tb, h * MLA_SLAB:(h + 1) * MLA_SLAB]
        q = query_ref[:, h * MLA_SLAB:(h + 1) * MLA_SLAB]
        return lax.dot_general(kb, q, _NT, preferred_element_type=_F32)

    @pl.when(i == 0)
    def _():
        for h in range(MLA_HEADS):
            s = first_scores(q_ref, h)
            s_sc[h] = s
            smax_sc[h] = jnp.max(s, axis=0, keepdims=True)

    def all_heads(j, valid, prefetch_next):
        if prefetch_next:
            s_next = [scores(j + 1, h) for h in range(MLA_HEADS)]
        else:
            s_next = [first_scores(qnext_ref, h) for h in range(MLA_HEADS)]
        new_m, new_l, new_acc = [], [], []
        for h in range(MLA_HEADS):
            m_old = m_sc[h]
            if valid is None:
                m_new = jnp.maximum(m_old, smax_sc[h])
                p = jnp.exp2(s_sc[h] - m_new)
            else:
                s = jnp.where(valid, s_sc[h], -jnp.inf)
                m_new = jnp.maximum(m_old, jnp.max(s, axis=0, keepdims=True))
                p = jnp.exp2(s - m_new)
            alpha = jnp.exp2(m_old - m_new)
            new_l.append(alpha * l_sc[h] + jnp.sum(p, axis=0, keepdims=True))
            new_m.append(m_new)
            p_sc[h] = p.astype(_BF)
            pv = None
            for c in range(tb // sub):
                vt = vt_ref[0, j * (tb // sub) + c, h * MLA_V_DIM:(h + 1) * MLA_V_DIM, :]
                part = jnp.dot(vt, p_sc[h, c * sub:(c + 1) * sub, :], preferred_element_type=_F32)
                pv = part if pv is None else pv + part
            new_acc.append(alpha * acc_sc[h] + pv)
        for h in range(MLA_HEADS):
            m_sc[h] = new_m[h]
            l_sc[h] = new_l[h]
            acc_sc[h] = new_acc[h]
        for h, s in enumerate(s_next):
            s_sc[h] = s
            smax_sc[h] = jnp.max(s, axis=0, keepdims=True)

    def body(j, c):
        all_heads(j, None, True)
        return c

    lax.fori_loop(0, i, body, 0)
    all_heads(i, causal, False)
    out_t = jnp.concatenate([acc_sc[h] / l_sc[h] for h in range(MLA_HEADS)], axis=0)
    o_ref[...] = out_t.T


def _mla_attention(mq, mk, mvt, batch, seq):
    tb = MLA_TILE
    nblk = seq // tb
    return pl.pallas_call(
        _mla_kernel,
        grid=(batch, nblk),
        in_specs=[
            pl.BlockSpec((tb, MLA_HEADS * MLA_SLAB), lambda b, i: (b * nblk + i, 0)),
            pl.BlockSpec((tb, MLA_HEADS * MLA_SLAB),
                         lambda b, i: (b * nblk + jnp.minimum(i + 1, nblk - 1), 0)),
            pl.BlockSpec((seq, MLA_HEADS * MLA_SLAB), lambda b, i: (b, 0),
                         pipeline_mode=pl.Buffered(1)),
            pl.BlockSpec((1,) + mvt.shape[1:], lambda b, i: (b, 0, 0, 0),
                         pipeline_mode=pl.Buffered(1)),
        ],
        out_specs=pl.BlockSpec((tb, MLA_WIDTH), lambda b, i: (b * nblk + i, 0)),
        out_shape=jax.ShapeDtypeStruct((batch * seq, MLA_WIDTH), _F32),
        scratch_shapes=[pltpu.VMEM((MLA_HEADS, 1, tb), _F32), pltpu.VMEM((MLA_HEADS, 1, tb), _F32),
                        pltpu.VMEM((MLA_HEADS, MLA_V_DIM, tb), _F32),
                        pltpu.VMEM((MLA_HEADS, tb, tb), _F32),
                        pltpu.VMEM((MLA_HEADS, 1, tb), _F32),
                        pltpu.VMEM((MLA_HEADS, tb, tb), _BF)],
        compiler_params=_params("arbitrary", "arbitrary"),
        name="mla_attention",
    )(mq, mq, mk, mvt)


def _post_kernel(h_ref, sb_ref, mla_ref, p_ref, gsb_ref, gmla_ref, wout_ref,
                 gffn_ref, wg_ref, wu_ref, wd_ref, gple_ref, wpg_ref, wpp_ref, gfin_ref,
                 o_ref, *, final_norm):
    sb = _rms(sb_ref[...], gsb_ref[...]).astype(_BF)
    ml = _rms(mla_ref[...], gmla_ref[...]).astype(_BF)
    h1 = h_ref[...] + jnp.dot(sb, wout_ref[:SB_WIDTH, :], preferred_element_type=_F32)
    h1 = h1 + jnp.dot(ml, wout_ref[SB_WIDTH:, :], preferred_element_type=_F32)
    h2 = _swiglu_half_step(h1, gffn_ref, wg_ref, wu_ref, wd_ref)
    hn = _rms(h2, gple_ref[...]).astype(_BF)
    gate = jax.nn.sigmoid(jnp.dot(hn, wpg_ref[...], preferred_element_type=_F32))
    emb = jnp.dot(p_ref[...].astype(_BF), wpp_ref[...], preferred_element_type=_F32)
    out = h2 + gate * emb
    if final_norm:
        out = _rms(out, gfin_ref[...])
    o_ref[...] = out


def _post_mixer(h, sb, mla, p, li, gsb, gmla, wout, gffn, wg, wu, wd, gple, wpg, wpp, gfin,
                final_norm):
    t = h.shape[0]
    tm = POST_TILE
    row = lambda n: pl.BlockSpec((tm, n), lambda i: (i, 0))
    layered = (gsb, gmla, wout, gffn, wg, wu, wd, gple, wpg, wpp)
    return pl.pallas_call(
        functools.partial(_post_kernel, final_norm=final_norm),
        grid=(t // tm,),
        in_specs=[row(D_MODEL), row(SB_WIDTH), row(MLA_WIDTH),
                  pl.BlockSpec((None, tm, D_PLE), lambda i: (li, i, 0))]
                 + [_layer(a, li) for a in layered] + [_resident(gfin.shape)],
        out_specs=row(D_MODEL),
        out_shape=jax.ShapeDtypeStruct(h.shape, _F32),
        compiler_params=_params("parallel"),
        name="post_mixer",
    )(h, sb, mla, p, *layered, gfin)


def _swap_cols(w):
    half = w.shape[-1] // 2
    return jnp.concatenate([-w[..., half:], w[..., :half]], axis=-1)


def _mixer_weights(w_in, q_lat_norm, w_uq, kv_lat_norm, w_ukv):
    depth = w_in.shape[0]
    o1, o2, o3 = SB_WIDTH, 2 * SB_WIDTH, 3 * SB_WIDTH
    o4 = o3 + Q_LORA
    o5 = o4 + KV_LORA
    w_kr = w_in[..., o5:]
    wlat = jnp.concatenate([w_in[..., o3:o5], w_kr, _swap_cols(w_kr)], axis=-1)
    uq = w_uq.reshape(depth, Q_LORA, MLA_HEADS, MLA_QK_DIM)
    wuq = jnp.concatenate([uq, _swap_cols(uq[..., MLA_NOPE_DIM:])], axis=-1)
    ukv = w_ukv.reshape(depth, KV_LORA, MLA_HEADS, MLA_NOPE_DIM + MLA_V_DIM)
    transposed = lambda w: jnp.swapaxes(w, -1, -2).astype(_BF)
    return {
        "wq": w_in[..., :o1].astype(_BF),
        "wk": w_in[..., o1:o2].astype(_BF),
        "wvt": transposed(w_in[..., o2:o3]),
        "wlat": wlat.astype(_BF),
        "gq": q_lat_norm.reshape(depth, 1, -1),
        "gkv": kv_lat_norm.reshape(depth, 1, -1),
        "wuq": wuq.reshape(depth, Q_LORA, -1).astype(_BF),
        "wuk": ukv[..., :MLA_NOPE_DIM].reshape(depth, KV_LORA, -1).astype(_BF),
        "wuvt": transposed(ukv[..., MLA_NOPE_DIM:].reshape(depth, KV_LORA, -1)),
    }


def _rope_tables(seq):
    half = MLA_ROPE_DIM // 2
    inv_freq = ROPE_THETA ** (-np.arange(half, dtype=np.float64) / half)
    ang = np.arange(seq, dtype=np.float64)[:, None] * inv_freq[None, :]
    pad = np.zeros((seq, LANES - MLA_ROPE_DIM))
    table = lambda f: jnp.asarray(np.concatenate([f(ang), f(ang), pad], axis=1), _F32)
    return table(np.cos), table(np.sin)


def kernel(x, p, ffn1_norm, ffn1_w_gate, ffn1_w_up, ffn1_w_down, mix_norm, w_in, q_lat_norm, w_uq, kv_lat_norm, w_ukv, sb_out_norm, mla_out_norm, w_out, ffn2_norm, ffn2_w_gate, ffn2_w_up, ffn2_w_down, ple_norm, w_ple_gate, w_ple_proj, final_norm):
    batch, seq, _ = x.shape
    depth = p.shape[0]
    assert seq % MLA_TILE == 0 and seq % ATT_TILE == 0 and (batch * seq) % TOK_TILE == 0
    t = batch * seq
    cos2, sin2 = _rope_tables(seq)
    idx = np.arange(ATT_TILE)
    tneg = jnp.asarray(np.where(idx[None, :] > idx[:, None], -1.0, 0.0), _BF)
    vec = lambda v: v.reshape(1, -1)
    gains = lambda g: g.reshape(depth, 1, -1)
    p_tok = p.reshape(depth, t, D_PLE)
    w_out_bf, w_ple_gate_bf, w_ple_proj_bf = (w.astype(_BF) for w in (w_out, w_ple_gate, w_ple_proj))
    mw = _mixer_weights(w_in, q_lat_norm, w_uq, kv_lat_norm, w_ukv)

    h = x.reshape(t, D_MODEL)
    for li in range(depth):
        h = _ffn(h, li, gains(ffn1_norm), ffn1_w_gate, ffn1_w_up, ffn1_w_down)
        sbq, sbk, sbvt, mq, mk, mvt = _proj(h, li, gains(mix_norm), mw, cos2, sin2, batch, seq)
        sb = _sb_attention(sbq, sbk, sbvt, tneg, batch, seq)
        mla = _mla_attention(mq, mk, mvt, batch, seq)
        h = _post_mixer(h, sb, mla, p_tok, li, gains(sb_out_norm), gains(mla_out_norm), w_out_bf,
                        gains(ffn2_norm), ffn2_w_gate, ffn2_w_up, ffn2_w_down,
                        gains(ple_norm), w_ple_gate_bf, w_ple_proj_bf, vec(final_norm),
                        li == depth - 1)
    return h.reshape(batch, seq, D_MODEL)
```

```python
import functools
import math

import jax
import jax.numpy as jnp
import numpy as np
from jax import lax
from jax.experimental import pallas as pl
from jax.experimental.pallas import tpu as pltpu

D_MODEL = 1024
D_PLE = 256
SB_HEADS = 8
SB_HEAD_DIM = 64
SB_WIDTH = SB_HEADS * SB_HEAD_DIM
MLA_HEADS = 4
MLA_NOPE_DIM = 128
MLA_ROPE_DIM = 64
MLA_QK_DIM = MLA_NOPE_DIM + MLA_ROPE_DIM
MLA_V_DIM = 128
MLA_WIDTH = MLA_HEADS * MLA_V_DIM
Q_LORA = 256
KV_LORA = 128
D_FF = 2816
ROPE_THETA = 10000.0
EPS = 1e-6

LANES = 128
MXU_DIM = 256
VMEM_LIMIT_BYTES = 58 * 1024 * 1024

FF_CHUNK = MXU_DIM
N_FF_CHUNKS = D_FF // FF_CHUNK
TOK_TILE = 512
POST_TILE = 512
ATT_TILE = 256
MLA_TILE = 512
MLA_SLAB = 2 * LANES
MLA_ONES_ROWS = 16
SB_DEAD_LOG2 = -152.0
SB_MASKED_LOGIT = -1e30

_BF = jnp.bfloat16
_F32 = jnp.float32
_NT = (((1,), (1,)), ((), ()))


def _rms(x, g):
    return x * lax.rsqrt(jnp.mean(x * x, axis=-1, keepdims=True) + EPS) * g


def _params(*sem):
    return pltpu.CompilerParams(dimension_semantics=sem, vmem_limit_bytes=VMEM_LIMIT_BYTES)


def _resident(shape):
    zeros = (0,) * len(shape)
    return pl.BlockSpec(shape, lambda *_: zeros, pipeline_mode=pl.Buffered(1))


def _layer(arr, li):
    index = (li,) + (0,) * (arr.ndim - 1)
    return pl.BlockSpec((None,) + arr.shape[1:], lambda *_: index, pipeline_mode=pl.Buffered(1))


def _swiglu_half_step(x, g_ref, wg_ref, wu_ref, wd_ref):
    xn = _rms(x, g_ref[...]).astype(_BF)
    acc = jnp.zeros(x.shape, _F32)
    for c in range(N_FF_CHUNKS):
        cols = slice(c * FF_CHUNK, (c + 1) * FF_CHUNK)
        gate = jnp.dot(xn, wg_ref[:, cols].astype(_BF), preferred_element_type=_F32)
        up = jnp.dot(xn, wu_ref[:, cols].astype(_BF), preferred_element_type=_F32)
        hid = (gate * jax.nn.sigmoid(gate) * up).astype(_BF)
        acc = acc + jnp.dot(hid, wd_ref[cols, :].astype(_BF), preferred_element_type=_F32)
    return x + 0.5 * acc


def _ffn_kernel(x_ref, g_ref, wg_ref, wu_ref, wd_ref, o_ref):
    o_ref[...] = _swiglu_half_step(x_ref[...], g_ref, wg_ref, wu_ref, wd_ref)


def _ffn(x, li, g, wg, wu, wd):
    t = x.shape[0]
    row = pl.BlockSpec((TOK_TILE, D_MODEL), lambda i: (i, 0))
    return pl.pallas_call(
        _ffn_kernel,
        grid=(t // TOK_TILE,),
        in_specs=[row, _layer(g, li), _layer(wg, li), _layer(wu, li), _layer(wd, li)],
        out_specs=row,
        out_shape=jax.ShapeDtypeStruct(x.shape, _F32),
        compiler_params=_params("parallel"),
        name="ffn",
    )(x, g, wg, wu, wd)


def _rope_slab(x, cos, sin):
    return x * cos + pltpu.roll(x, LANES // 2, axis=1) * sin


def _proj_kernel(h_ref, g_ref, wq_ref, wk_ref, wvt_ref, wlat_ref, gq_ref, gkv_ref,
                 wuq_ref, wuk_ref, wuvt_ref, cos_ref, sin_ref,
                 sbq_ref, sbk_ref, sbvt_ref, mq_ref, mk_ref, mvt_ref):
    u = _rms(h_ref[...], g_ref[...]).astype(_BF)
    sb_scale = SB_HEAD_DIM ** -0.5 * math.log2(math.e)
    sbq_ref[...] = (jnp.dot(u, wq_ref[...], preferred_element_type=_F32) * sb_scale).astype(_BF)
    sbk_ref[...] = jnp.dot(u, wk_ref[...], preferred_element_type=_F32).astype(_BF)
    sbvt = lax.dot_general(wvt_ref[...], u, _NT, preferred_element_type=_F32).astype(_BF)
    for c in range(sbvt_ref.shape[1]):
        sbvt_ref[0, c] = sbvt[:, c * ATT_TILE:(c + 1) * ATT_TILE]

    lat = jnp.dot(u, wlat_ref[...], preferred_element_type=_F32)
    cos = cos_ref[...]
    sin = sin_ref[...]
    cq = _rms(lat[:, :Q_LORA], gq_ref[...]).astype(_BF)
    ckv = _rms(lat[:, Q_LORA:Q_LORA + KV_LORA], gkv_ref[...]).astype(_BF)
    o_kr = Q_LORA + KV_LORA
    k_rope = _rope_slab(lat[:, o_kr:o_kr + LANES], cos, sin).astype(_BF)

    qa = jnp.dot(cq, wuq_ref[...], preferred_element_type=_F32)
    kn = jnp.dot(ckv, wuk_ref[...], preferred_element_type=_F32)
    mvt = lax.dot_general(wuvt_ref[...], ckv, _NT, preferred_element_type=_F32).astype(_BF)
    for c in range(mvt_ref.shape[1]):
        mvt_ref[0, c] = mvt[:, c * ATT_TILE:(c + 1) * ATT_TILE]
    scale = MLA_QK_DIM ** -0.5 * math.log2(math.e)
    for hd in range(MLA_HEADS):
        o = hd * MLA_SLAB
        mq_ref[:, o:o + LANES] = (qa[:, o:o + LANES] * scale).astype(_BF)
        roped = _rope_slab(qa[:, o + LANES:o + 2 * LANES], cos, sin)
        mq_ref[:, o + LANES:o + 2 * LANES] = (roped * scale).astype(_BF)
        mk_ref[:, o:o + LANES] = kn[:, hd * LANES:(hd + 1) * LANES].astype(_BF)
        mk_ref[:, o + LANES:o + 2 * LANES] = k_rope


def _proj(h, li, g, w, cos2, sin2, batch, seq):
    t = batch * seq
    tm = TOK_TILE
    nblk = seq // tm
    kb = ATT_TILE
    row = lambda n: pl.BlockSpec((tm, n), lambda i: (i, 0))
    outs = (
        jax.ShapeDtypeStruct((t, SB_WIDTH), _BF),
        jax.ShapeDtypeStruct((t, SB_WIDTH), _BF),
        jax.ShapeDtypeStruct((batch, seq // kb, SB_WIDTH, kb), _BF),
        jax.ShapeDtypeStruct((t, MLA_HEADS * MLA_SLAB), _BF),
        jax.ShapeDtypeStruct((t, MLA_HEADS * MLA_SLAB), _BF),
        jax.ShapeDtypeStruct((batch, seq // kb, MLA_WIDTH, kb), _BF),
    )
    weights = (g, w["wq"], w["wk"], w["wvt"], w["wlat"], w["gq"], w["gkv"],
               w["wuq"], w["wuk"], w["wuvt"])
    vt_spec = pl.BlockSpec((1, tm // kb, SB_WIDTH, kb), lambda i: (i // nblk, i % nblk, 0, 0))
    rope_spec = pl.BlockSpec((tm, LANES), lambda i: (i % nblk, 0))
    return pl.pallas_call(
        _proj_kernel,
        grid=(t // tm,),
        in_specs=[row(D_MODEL)] + [_layer(a, li) for a in weights] + [rope_spec, rope_spec],
        out_specs=(row(SB_WIDTH), row(SB_WIDTH), vt_spec,
                   row(MLA_HEADS * MLA_SLAB), row(MLA_HEADS * MLA_SLAB), vt_spec),
        out_shape=outs,
        compiler_params=_params("parallel"),
        name="mixer_proj",
    )(h, *weights, cos2, sin2)


def _sb_kernel(q_ref, k_ref, vt_ref, tneg_ref, o_ref, carry_sc, acc_sc, spb_sc, lsig_sc, ab_sc):
    tb = ATT_TILE
    hd = SB_HEAD_DIM
    i = pl.program_id(1)
    lane = lax.broadcasted_iota(jnp.int32, (tb, LANES), 1)
    tneg = tneg_ref[...]
    key_row = lax.broadcasted_iota(jnp.int32, (tb, SB_HEADS * tb), 0)
    qry_col = lax.broadcasted_iota(jnp.int32, (tb, SB_HEADS * tb), 1) & (tb - 1)
    valid = key_row < qry_col

    carry_sc[...] = jnp.zeros(carry_sc.shape, _F32)
    acc_sc[...] = jnp.zeros(acc_sc.shape, _F32)

    q_pairs = []
    for pair in range(SB_HEADS // 2):
        q = q_ref[:, pair * LANES:(pair + 1) * LANES]
        zero = jnp.zeros_like(q)
        q_pairs.append(jnp.concatenate([jnp.where(lane < hd, q, zero),
                                        jnp.where(lane >= hd, q, zero)], axis=0))

    def key_blocks(blocks):
        operands = []
        for j, _ in blocks:
            rows = pl.ds(pl.multiple_of(j * tb, tb), tb)
            operands.append((
                [k_ref[rows, pair * LANES:(pair + 1) * LANES] for pair in range(SB_HEADS // 2)],
                [vt_ref[0, j, h * hd:(h + 1) * hd, :] for h in range(SB_HEADS)]))
        carry = carry_sc[...]
        total = None
        for slot, ((_, mask), (kbs, vbs)) in enumerate(zip(blocks, operands)):
            z = jnp.concatenate([lax.dot_general(kb, qp, _NT, preferred_element_type=_F32)
                                 for kb, qp in zip(kbs, q_pairs)], axis=1)
            if mask is not None:
                z = jnp.where(mask, z, SB_MASKED_LOGIT)
            sp = jnp.maximum(z, 0.0) + jnp.log2(1.0 + jnp.exp2(-jnp.abs(z)))
            spb_sc[slot] = sp.astype(_BF)
            lsig_sc[slot] = z - sp
            tail = jnp.dot(tneg, spb_sc[slot], preferred_element_type=_F32)
            ab_sc[slot] = jnp.exp2(lsig_sc[slot] + tail + carry).astype(_BF)
            contrib = jnp.concatenate(
                [jnp.dot(vbs[h], ab_sc[slot, :, h * tb:(h + 1) * tb], preferred_element_type=_F32)
                 for h in range(SB_HEADS)], axis=0)
            carry = carry + tail[0:1, :] - sp[0:1, :]
            total = contrib if total is None else total + contrib
        carry_sc[...] = carry
        acc_sc[...] += total

    def alive():
        return (jnp.max(carry_sc[...]) > SB_DEAD_LOG2).astype(jnp.int32)

    @pl.when(i == 0)
    def _():
        key_blocks([(i, valid)])

    @pl.when(i > 0)
    def _():
        key_blocks([(i, valid), (i - 1, None)])

    def cond(state):
        j, live = state
        return jnp.logical_and(j >= 0, live > 0)

    def body(state):
        j, _ = state
        key_blocks([(j, None)])
        return j - 1, alive()

    lax.while_loop(cond, body, (i - 2, alive()))
    o_ref[...] = acc_sc[...].T


def _sb_attention(sbq, sbk, sbvt, tneg, batch, seq):
    tb = ATT_TILE
    nblk = seq // tb
    return pl.pallas_call(
        _sb_kernel,
        grid=(batch, nblk),
        in_specs=[
            pl.BlockSpec((tb, SB_WIDTH), lambda b, i: (b * nblk + i, 0)),
            pl.BlockSpec((seq, SB_WIDTH), lambda b, i: (b, 0), pipeline_mode=pl.Buffered(1)),
            pl.BlockSpec((1, nblk, SB_WIDTH, tb), lambda b, i: (b, 0, 0, 0),
                         pipeline_mode=pl.Buffered(1)),
            _resident(tneg.shape),
        ],
        out_specs=pl.BlockSpec((tb, SB_WIDTH), lambda b, i: (b * nblk + i, 0)),
        out_shape=jax.ShapeDtypeStruct((batch * seq, SB_WIDTH), _F32),
        scratch_shapes=[pltpu.VMEM((1, SB_HEADS * tb), _F32), pltpu.VMEM((SB_WIDTH, tb), _F32),
                        pltpu.VMEM((2, tb, SB_HEADS * tb), _BF),
                        pltpu.VMEM((2, tb, SB_HEADS * tb), _F32),
                        pltpu.VMEM((2, tb, SB_HEADS * tb), _BF)],
        compiler_params=_params("parallel", "arbitrary"),
        name="sb_attention",
    )(sbq, sbk, sbvt, tneg)


def _mla_kernel(q_ref, qnext_ref, k_ref, vt_ref, o_ref, m_sc, acc_sc, s_sc, smax_sc, p_sc):
    tb = MLA_TILE
    sub = ATT_TILE
    i = pl.program_id(1)
    m_sc[...] = jnp.full(m_sc.shape, -1e30, _F32)
    acc_sc[...] = jnp.zeros(acc_sc.shape, _F32)
    key_row = lax.broadcasted_iota(jnp.int32, (tb, tb), 0)
    qry_col = lax.broadcasted_iota(jnp.int32, (tb, tb), 1)
    causal = key_row <= qry_col

    def scores(j, h):
        rows = pl.ds(pl.multiple_of(j * tb, tb), tb)
        kb = k_ref[rows, h * MLA_SLAB:(h + 1) * MLA_SLAB]
        q = q_ref[:, h * MLA_SLAB:(h + 1) * MLA_SLAB]
        return lax.dot_general(kb, q, _NT, preferred_element_type=_F32)

    def first_scores(query_ref, h):
        kb = k_ref[0:tb, h * MLA_SLAB:(h + 1) * MLA_SLAB]
        q = query_ref[:, h * MLA_SLAB:(h + 1) * MLA_SLAB]
        return lax.dot_general(kb, q, _NT, preferred_element_type=_F32)

    @pl.when(i == 0)
    def _():
        for h in range(MLA_HEADS):
            s = first_scores(q_ref, h)
            s_sc[h] = s
            smax_sc[h] = jnp.max(s, axis=0, keepdims=True)

    def all_heads(j, valid, prefetch_next):
        if prefetch_next:
            s_next = [scores(j + 1, h) for h in range(MLA_HEADS)]
        else:
            s_next = [first_scores(qnext_ref, h) for h in range(MLA_HEADS)]
        new_m, new_acc = [], []
        ones_rows = jnp.ones((MLA_ONES_ROWS, sub), _BF)
        for h in range(MLA_HEADS):
            m_old = m_sc[h]
            if valid is None:
                m_new = jnp.maximum(m_old, smax_sc[h])
                p = jnp.exp2(s_sc[h] - m_new)
            else:
                s = jnp.where(valid, s_sc[h], -jnp.inf)
                m_new = jnp.maximum(m_old, jnp.max(s, axis=0, keepdims=True))
                p = jnp.exp2(s - m_new)
            alpha = jnp.exp2(m_old - m_new)
            new_m.append(m_new)
            p_sc[h] = p.astype(_BF)
            pv = None
            for c in range(tb // sub):
                vt = vt_ref[0, j * (tb // sub) + c, h * MLA_V_DIM:(h + 1) * MLA_V_DIM, :]
                vt1 = jnp.concatenate([vt, ones_rows], axis=0)
                part = jnp.dot(vt1, p_sc[h, c * sub:(c + 1) * sub, :], preferred_element_type=_F32)
                pv = part if pv is None else pv + part
            new_acc.append(alpha * acc_sc[h] + pv)
        for h in range(MLA_HEADS):
            m_sc[h] = new_m[h]
            acc_sc[h] = new_acc[h]
        for h, s in enumerate(s_next):
            s_sc[h] = s
            smax_sc[h] = jnp.max(s, axis=0, keepdims=True)

    def body(j, c):
        all_heads(j, None, True)
        return c

    lax.fori_loop(0, i, body, 0)
    all_heads(i, causal, False)
    out_t = jnp.concatenate(
        [acc_sc[h, :MLA_V_DIM, :] / acc_sc[h, MLA_V_DIM:MLA_V_DIM + 1, :] for h in range(MLA_HEADS)],
        axis=0)
    o_ref[...] = out_t.T


def _mla_attention(mq, mk, mvt, batch, seq):
    tb = MLA_TILE
    nblk = seq // tb
    return pl.pallas_call(
        _mla_kernel,
        grid=(batch, nblk),
        in_specs=[
            pl.BlockSpec((tb, MLA_HEADS * MLA_SLAB), lambda b, i: (b * nblk + i, 0)),
            pl.BlockSpec((tb, MLA_HEADS * MLA_SLAB),
                         lambda b, i: (b * nblk + jnp.minimum(i + 1, nblk - 1), 0)),
            pl.BlockSpec((seq, MLA_HEADS * MLA_SLAB), lambda b, i: (b, 0),
                         pipeline_mode=pl.Buffered(1)),
            pl.BlockSpec((1,) + mvt.shape[1:], lambda b, i: (b, 0, 0, 0),
                         pipeline_mode=pl.Buffered(1)),
        ],
        out_specs=pl.BlockSpec((tb, MLA_WIDTH), lambda b, i: (b * nblk + i, 0)),
        out_shape=jax.ShapeDtypeStruct((batch * seq, MLA_WIDTH), _F32),
        scratch_shapes=[pltpu.VMEM((MLA_HEADS, 1, tb), _F32),
                        pltpu.VMEM((MLA_HEADS, MLA_V_DIM + MLA_ONES_ROWS, tb), _F32),
                        pltpu.VMEM((MLA_HEADS, tb, tb), _F32),
                        pltpu.VMEM((MLA_HEADS, 1, tb), _F32),
                        pltpu.VMEM((MLA_HEADS, tb, tb), _BF)],
        compiler_params=_params("arbitrary", "arbitrary"),
        name="mla_attention",
    )(mq, mq, mk, mvt)


def _post_kernel(h_ref, sb_ref, mla_ref, p_ref, gsb_ref, gmla_ref, wout_ref,
                 gffn_ref, wg_ref, wu_ref, wd_ref, gple_ref, wpg_ref, wpp_ref, gfin_ref,
                 o_ref, *, final_norm):
    sb = _rms(sb_ref[...], gsb_ref[...]).astype(_BF)
    ml = _rms(mla_ref[...], gmla_ref[...]).astype(_BF)
    h1 = h_ref[...] + jnp.dot(sb, wout_ref[:SB_WIDTH, :], preferred_element_type=_F32)
    h1 = h1 + jnp.dot(ml, wout_ref[SB_WIDTH:, :], preferred_element_type=_F32)
    h2 = _swiglu_half_step(h1, gffn_ref, wg_ref, wu_ref, wd_ref)
    hn = _rms(h2, gple_ref[...]).astype(_BF)
    gate = jax.nn.sigmoid(jnp.dot(hn, wpg_ref[...], preferred_element_type=_F32))
    emb = jnp.dot(p_ref[...].astype(_BF), wpp_ref[...], preferred_element_type=_F32)
    out = h2 + gate * emb
    if final_norm:
        out = _rms(out, gfin_ref[...])
    o_ref[...] = out


def _post_mixer(h, sb, mla, p, li, gsb, gmla, wout, gffn, wg, wu, wd, gple, wpg, wpp, gfin,
                final_norm):
    t = h.shape[0]
    tm = POST_TILE
    row = lambda n: pl.BlockSpec((tm, n), lambda i: (i, 0))
    layered = (gsb, gmla, wout, gffn, wg, wu, wd, gple, wpg, wpp)
    return pl.pallas_call(
        functools.partial(_post_kernel, final_norm=final_norm),
        grid=(t // tm,),
        in_specs=[row(D_MODEL), row(SB_WIDTH), row(MLA_WIDTH),
                  pl.BlockSpec((None, tm, D_PLE), lambda i: (li, i, 0))]
                 + [_layer(a, li) for a in layered] + [_resident(gfin.shape)],
        out_specs=row(D_MODEL),
        out_shape=jax.ShapeDtypeStruct(h.shape, _F32),
        compiler_params=_params("parallel"),
        name="post_mixer",
    )(h, sb, mla, p, *layered, gfin)


def _swap_cols(w):
    half = w.shape[-1] // 2
    return jnp.concatenate([-w[..., half:], w[..., :half]], axis=-1)


def _mixer_weights(w_in, q_lat_norm, w_uq, kv_lat_norm, w_ukv):
    depth = w_in.shape[0]
    o1, o2, o3 = SB_WIDTH, 2 * SB_WIDTH, 3 * SB_WIDTH
    o4 = o3 + Q_LORA
    o5 = o4 + KV_LORA
    w_kr = w_in[..., o5:]
    wlat = jnp.concatenate([w_in[..., o3:o5], w_kr, _swap_cols(w_kr)], axis=-1)
    uq = w_uq.reshape(depth, Q_LORA, MLA_HEADS, MLA_QK_DIM)
    wuq = jnp.concatenate([uq, _swap_cols(uq[..., MLA_NOPE_DIM:])], axis=-1)
    ukv = w_ukv.reshape(depth, KV_LORA, MLA_HEADS, MLA_NOPE_DIM + MLA_V_DIM)
    transposed = lambda w: jnp.swapaxes(w, -1, -2).astype(_BF)
    return {
        "wq": w_in[..., :o1].astype(_BF),
        "wk": w_in[..., o1:o2].astype(_BF),
        "wvt": transposed(w_in[..., o2:o3]),
        "wlat": wlat.astype(_BF),
        "gq": q_lat_norm.reshape(depth, 1, -1),
        "gkv": kv_lat_norm.reshape(depth, 1, -1),
        "wuq": wuq.reshape(depth, Q_LORA, -1).astype(_BF),
        "wuk": ukv[..., :MLA_NOPE_DIM].reshape(depth, KV_LORA, -1).astype(_BF),
        "wuvt": transposed(ukv[..., MLA_NOPE_DIM:].reshape(depth, KV_LORA, -1)),
    }


def _rope_tables(seq):
    half = MLA_ROPE_DIM // 2
    inv_freq = ROPE_THETA ** (-np.arange(half, dtype=np.float64) / half)
    ang = np.arange(seq, dtype=np.float64)[:, None] * inv_freq[None, :]
    pad = np.zeros((seq, LANES - MLA_ROPE_DIM))
    table = lambda f: jnp.asarray(np.concatenate([f(ang), f(ang), pad], axis=1), _F32)
    return table(np.cos), table(np.sin)


def kernel(x, p, ffn1_norm, ffn1_w_gate, ffn1_w_up, ffn1_w_down, mix_norm, w_in, q_lat_norm, w_uq, kv_lat_norm, w_ukv, sb_out_norm, mla_out_norm, w_out, ffn2_norm, ffn2_w_gate, ffn2_w_up, ffn2_w_down, ple_norm, w_ple_gate, w_ple_proj, final_norm):
    batch, seq, _ = x.shape
    depth = p.shape[0]
    assert seq % MLA_TILE == 0 and seq % ATT_TILE == 0 and (batch * seq) % TOK_TILE == 0
    t = batch * seq
    cos2, sin2 = _rope_tables(seq)
    idx = np.arange(ATT_TILE)
    tneg = jnp.asarray(np.where(idx[None, :] > idx[:, None], -1.0, 0.0), _BF)
    vec = lambda v: v.reshape(1, -1)
    gains = lambda g: g.reshape(depth, 1, -1)
    p_tok = p.reshape(depth, t, D_PLE)
    w_out_bf, w_ple_gate_bf, w_ple_proj_bf = (w.astype(_BF) for w in (w_out, w_ple_gate, w_ple_proj))
    mw = _mixer_weights(w_in, q_lat_norm, w_uq, kv_lat_norm, w_ukv)

    h = x.reshape(t, D_MODEL)
    for li in range(depth):
        h = _ffn(h, li, gains(ffn1_norm), ffn1_w_gate, ffn1_w_up, ffn1_w_down)
        sbq, sbk, sbvt, mq, mk, mvt = _proj(h, li, gains(mix_norm), mw, cos2, sin2, batch, seq)
        sb = _sb_attention(sbq, sbk, sbvt, tneg, batch, seq)
        mla = _mla_attention(mq, mk, mvt, batch, seq)
        h = _post_mixer(h, sb, mla, p_tok, li, gains(sb_out_norm), gains(mla_out_norm), w_out_bf,
                        gains(ffn2_norm), ffn2_w_gate, ffn2_w_up, ffn2_w_down,
                        gains(ple_norm), w_ple_gate_bf, w_ple_proj_bf, vec(final_norm),
                        li == depth - 1)
    return h.reshape(batch, seq, D_MODEL)
```

```python
import functools
import math

import jax
import jax.numpy as jnp
import numpy as np
from jax import lax
from jax.experimental import pallas as pl
from jax.experimental.pallas import tpu as pltpu

D_MODEL = 1024
D_PLE = 256
SB_HEADS = 8
SB_HEAD_DIM = 64
SB_WIDTH = SB_HEADS * SB_HEAD_DIM
MLA_HEADS = 4
MLA_NOPE_DIM = 128
MLA_ROPE_DIM = 64
MLA_QK_DIM = MLA_NOPE_DIM + MLA_ROPE_DIM
MLA_V_DIM = 128
MLA_WIDTH = MLA_HEADS * MLA_V_DIM
Q_LORA = 256
KV_LORA = 128
D_FF = 2816
ROPE_THETA = 10000.0
EPS = 1e-6

LANES = 128
MXU_DIM = 256
VMEM_LIMIT_BYTES = 58 * 1024 * 1024

FF_CHUNK = MXU_DIM
N_FF_CHUNKS = D_FF // FF_CHUNK
TOK_TILE = 512
POST_TILE = 512
PROJ_TILE = 1024
ATT_TILE = 256
MLA_TILE = 512
MLA_SLAB = 2 * LANES
MLA_ONES_ROWS = 16
SB_DEAD_LOG2 = -152.0
SB_MASKED_LOGIT = -1e30

_BF = jnp.bfloat16
_F32 = jnp.float32
_NT = (((1,), (1,)), ((), ()))


def _rms(x, g):
    return x * lax.rsqrt(jnp.mean(x * x, axis=-1, keepdims=True) + EPS) * g


def _params(*sem):
    return pltpu.CompilerParams(dimension_semantics=sem, vmem_limit_bytes=VMEM_LIMIT_BYTES)


def _resident(shape):
    zeros = (0,) * len(shape)
    return pl.BlockSpec(shape, lambda *_: zeros, pipeline_mode=pl.Buffered(1))


def _layer(arr, li):
    index = (li,) + (0,) * (arr.ndim - 1)
    return pl.BlockSpec((None,) + arr.shape[1:], lambda *_: index, pipeline_mode=pl.Buffered(1))


def _swiglu_half_step(x, g_ref, wg_ref, wu_ref, wd_ref):
    xn = _rms(x, g_ref[...]).astype(_BF)
    acc = jnp.zeros(x.shape, _F32)
    for c in range(N_FF_CHUNKS):
        cols = slice(c * FF_CHUNK, (c + 1) * FF_CHUNK)
        gate = jnp.dot(xn, wg_ref[:, cols].astype(_BF), preferred_element_type=_F32)
        up = jnp.dot(xn, wu_ref[:, cols].astype(_BF), preferred_element_type=_F32)
        hid = (gate * jax.nn.sigmoid(gate) * up).astype(_BF)
        acc = acc + jnp.dot(hid, wd_ref[cols, :].astype(_BF), preferred_element_type=_F32)
    return x + 0.5 * acc


def _ffn_kernel(x_ref, g_ref, wg_ref, wu_ref, wd_ref, o_ref):
    o_ref[...] = _swiglu_half_step(x_ref[...], g_ref, wg_ref, wu_ref, wd_ref)


def _ffn(x, li, g, wg, wu, wd):
    t = x.shape[0]
    row = pl.BlockSpec((TOK_TILE, D_MODEL), lambda i: (i, 0))
    return pl.pallas_call(
        _ffn_kernel,
        grid=(t // TOK_TILE,),
        in_specs=[row, _layer(g, li), _layer(wg, li), _layer(wu, li), _layer(wd, li)],
        out_specs=row,
        out_shape=jax.ShapeDtypeStruct(x.shape, _F32),
        compiler_params=_params("parallel"),
        name="ffn",
    )(x, g, wg, wu, wd)


def _rope_slab(x, cos, sin):
    return x * cos + pltpu.roll(x, LANES // 2, axis=1) * sin


def _proj_kernel(h_ref, g_ref, wq_ref, wk_ref, wvt_ref, wlat_ref, gq_ref, gkv_ref,
                 wuq_ref, wuk_ref, wuvt_ref, cos_ref, sin_ref,
                 sbq_ref, sbk_ref, sbvt_ref, mq_ref, mk_ref, mvt_ref):
    u = _rms(h_ref[...], g_ref[...]).astype(_BF)
    sb_scale = SB_HEAD_DIM ** -0.5 * math.log2(math.e)
    sbq_ref[...] = (jnp.dot(u, wq_ref[...], preferred_element_type=_F32) * sb_scale).astype(_BF)
    sbk_ref[...] = jnp.dot(u, wk_ref[...], preferred_element_type=_F32).astype(_BF)
    sbvt = lax.dot_general(wvt_ref[...], u, _NT, preferred_element_type=_F32).astype(_BF)
    for c in range(sbvt_ref.shape[1]):
        sbvt_ref[0, c] = sbvt[:, c * ATT_TILE:(c + 1) * ATT_TILE]

    lat = jnp.dot(u, wlat_ref[...], preferred_element_type=_F32)
    cos = cos_ref[...]
    sin = sin_ref[...]
    cq = _rms(lat[:, :Q_LORA], gq_ref[...]).astype(_BF)
    ckv = _rms(lat[:, Q_LORA:Q_LORA + KV_LORA], gkv_ref[...]).astype(_BF)
    o_kr = Q_LORA + KV_LORA
    k_rope = _rope_slab(lat[:, o_kr:o_kr + LANES], cos, sin).astype(_BF)

    qa = jnp.dot(cq, wuq_ref[...], preferred_element_type=_F32)
    kn = jnp.dot(ckv, wuk_ref[...], preferred_element_type=_F32)
    mvt = lax.dot_general(wuvt_ref[...], ckv, _NT, preferred_element_type=_F32).astype(_BF)
    for c in range(mvt_ref.shape[1]):
        mvt_ref[0, c] = mvt[:, c * ATT_TILE:(c + 1) * ATT_TILE]
    scale = MLA_QK_DIM ** -0.5 * math.log2(math.e)
    for hd in range(MLA_HEADS):
        o = hd * MLA_SLAB
        mq_ref[:, o:o + LANES] = (qa[:, o:o + LANES] * scale).astype(_BF)
        roped = _rope_slab(qa[:, o + LANES:o + 2 * LANES], cos, sin)
        mq_ref[:, o + LANES:o + 2 * LANES] = (roped * scale).astype(_BF)
        mk_ref[:, o:o + LANES] = kn[:, hd * LANES:(hd + 1) * LANES].astype(_BF)
        mk_ref[:, o + LANES:o + 2 * LANES] = k_rope


def _proj(h, li, g, w, cos2, sin2, batch, seq):
    t = batch * seq
    tm = PROJ_TILE
    nblk = seq // tm
    kb = ATT_TILE
    row = lambda n: pl.BlockSpec((tm, n), lambda i: (i, 0))
    outs = (
        jax.ShapeDtypeStruct((t, SB_WIDTH), _BF),
        jax.ShapeDtypeStruct((t, SB_WIDTH), _BF),
        jax.ShapeDtypeStruct((batch, seq // kb, SB_WIDTH, kb), _BF),
        jax.ShapeDtypeStruct((t, MLA_HEADS * MLA_SLAB), _BF),
        jax.ShapeDtypeStruct((t, MLA_HEADS * MLA_SLAB), _BF),
        jax.ShapeDtypeStruct((batch, seq // kb, MLA_WIDTH, kb), _BF),
    )
    weights = (g, w["wq"], w["wk"], w["wvt"], w["wlat"], w["gq"], w["gkv"],
               w["wuq"], w["wuk"], w["wuvt"])
    vt_spec = pl.BlockSpec((1, tm // kb, SB_WIDTH, kb), lambda i: (i // nblk, i % nblk, 0, 0))
    rope_spec = pl.BlockSpec((tm, LANES), lambda i: (i % nblk, 0))
    return pl.pallas_call(
        _proj_kernel,
        grid=(t // tm,),
        in_specs=[row(D_MODEL)] + [_layer(a, li) for a in weights] + [rope_spec, rope_spec],
        out_specs=(row(SB_WIDTH), row(SB_WIDTH), vt_spec,
                   row(MLA_HEADS * MLA_SLAB), row(MLA_HEADS * MLA_SLAB), vt_spec),
        out_shape=outs,
        compiler_params=_params("parallel"),
        name="mixer_proj",
    )(h, *weights, cos2, sin2)


def _sb_kernel(q_ref, k_ref, vt_ref, tneg_ref, o_ref, carry_sc, acc_sc, spb_sc, lsig_sc, ab_sc):
    tb = ATT_TILE
    hd = SB_HEAD_DIM
    i = pl.program_id(1)
    lane = lax.broadcasted_iota(jnp.int32, (tb, LANES), 1)
    tneg = tneg_ref[...]
    key_row = lax.broadcasted_iota(jnp.int32, (tb, SB_HEADS * tb), 0)
    qry_col = lax.broadcasted_iota(jnp.int32, (tb, SB_HEADS * tb), 1) & (tb - 1)
    valid = key_row < qry_col

    carry_sc[...] = jnp.zeros(carry_sc.shape, _F32)
    acc_sc[...] = jnp.zeros(acc_sc.shape, _F32)

    q_pairs = []
    for pair in range(SB_HEADS // 2):
        q = q_ref[:, pair * LANES:(pair + 1) * LANES]
        zero = jnp.zeros_like(q)
        q_pairs.append(jnp.concatenate([jnp.where(lane < hd, q, zero),
                                        jnp.where(lane >= hd, q, zero)], axis=0))

    def key_blocks(blocks):
        operands = []
        for j, _ in blocks:
            rows = pl.ds(pl.multiple_of(j * tb, tb), tb)
            operands.append((
                [k_ref[rows, pair * LANES:(pair + 1) * LANES] for pair in range(SB_HEADS // 2)],
                [vt_ref[0, j, h * hd:(h + 1) * hd, :] for h in range(SB_HEADS)]))
        carry = carry_sc[...]
        total = None
        for slot, ((_, mask), (kbs, vbs)) in enumerate(zip(blocks, operands)):
            z = jnp.concatenate([lax.dot_general(kb, qp, _NT, preferred_element_type=_F32)
                                 for kb, qp in zip(kbs, q_pairs)], axis=1)
            if mask is not None:
                z = jnp.where(mask, z, SB_MASKED_LOGIT)
            sp = jnp.maximum(z, 0.0) + jnp.log2(1.0 + jnp.exp2(-jnp.abs(z)))
            spb_sc[slot] = sp.astype(_BF)
            lsig_sc[slot] = z - sp
            tail = jnp.dot(tneg, spb_sc[slot], preferred_element_type=_F32)
            ab_sc[slot] = jnp.exp2(lsig_sc[slot] + tail + carry).astype(_BF)
            contrib = jnp.concatenate(
                [jnp.dot(vbs[h], ab_sc[slot, :, h * tb:(h + 1) * tb], preferred_element_type=_F32)
                 for h in range(SB_HEADS)], axis=0)
            carry = carry + tail[0:1, :] - sp[0:1, :]
            total = contrib if total is None else total + contrib
        carry_sc[...] = carry
        acc_sc[...] += total

    def alive():
        return (jnp.max(carry_sc[...]) > SB_DEAD_LOG2).astype(jnp.int32)

    @pl.when(i == 0)
    def _():
        key_blocks([(i, valid)])

    @pl.when(i > 0)
    def _():
        key_blocks([(i, valid), (i - 1, None)])

    def cond(state):
        j, live = state
        return jnp.logical_and(j >= 0, live > 0)

    def body(state):
        j, _ = state
        key_blocks([(j, None)])
        return j - 1, alive()

    lax.while_loop(cond, body, (i - 2, alive()))
    o_ref[...] = acc_sc[...].T


def _sb_attention(sbq, sbk, sbvt, tneg, batch, seq):
    tb = ATT_TILE
    nblk = seq // tb
    return pl.pallas_call(
        _sb_kernel,
        grid=(batch, nblk),
        in_specs=[
            pl.BlockSpec((tb, SB_WIDTH), lambda b, i: (b * nblk + i, 0)),
            pl.BlockSpec((seq, SB_WIDTH), lambda b, i: (b, 0), pipeline_mode=pl.Buffered(1)),
            pl.BlockSpec((1, nblk, SB_WIDTH, tb), lambda b, i: (b, 0, 0, 0),
                         pipeline_mode=pl.Buffered(1)),
            _resident(tneg.shape),
        ],
        out_specs=pl.BlockSpec((tb, SB_WIDTH), lambda b, i: (b * nblk + i, 0)),
        out_shape=jax.ShapeDtypeStruct((batch * seq, SB_WIDTH), _F32),
        scratch_shapes=[pltpu.VMEM((1, SB_HEADS * tb), _F32), pltpu.VMEM((SB_WIDTH, tb), _F32),
                        pltpu.VMEM((2, tb, SB_HEADS * tb), _BF),
                        pltpu.VMEM((2, tb, SB_HEADS * tb), _F32),
                        pltpu.VMEM((2, tb, SB_HEADS * tb), _BF)],
        compiler_params=_params("parallel", "arbitrary"),
        name="sb_attention",
    )(sbq, sbk, sbvt, tneg)


def _mla_kernel(q_ref, qnext_ref, k_ref, vt_ref, o_ref, m_sc, acc_sc, s_sc, smax_sc, p_sc):
    tb = MLA_TILE
    sub = ATT_TILE
    i = pl.program_id(1)
    m_sc[...] = jnp.full(m_sc.shape, -1e30, _F32)
    acc_sc[...] = jnp.zeros(acc_sc.shape, _F32)
    key_row = lax.broadcasted_iota(jnp.int32, (tb, tb), 0)
    qry_col = lax.broadcasted_iota(jnp.int32, (tb, tb), 1)
    causal = key_row <= qry_col

    def scores(j, h):
        rows = pl.ds(pl.multiple_of(j * tb, tb), tb)
        kb = k_ref[rows, h * MLA_SLAB:(h + 1) * MLA_SLAB]
        q = q_ref[:, h * MLA_SLAB:(h + 1) * MLA_SLAB]
        return lax.dot_general(kb, q, _NT, preferred_element_type=_F32)

    def first_scores(query_ref, h):
        kb = k_ref[0:tb, h * MLA_SLAB:(h + 1) * MLA_SLAB]
        q = query_ref[:, h * MLA_SLAB:(h + 1) * MLA_SLAB]
        return lax.dot_general(kb, q, _NT, preferred_element_type=_F32)

    @pl.when(i == 0)
    def _():
        for h in range(MLA_HEADS):
            s = first_scores(q_ref, h)
            s_sc[h] = s
            smax_sc[h] = jnp.max(s, axis=0, keepdims=True)

    def all_heads(j, valid, prefetch_next):
        if prefetch_next:
            s_next = [scores(j + 1, h) for h in range(MLA_HEADS)]
        else:
            s_next = [first_scores(qnext_ref, h) for h in range(MLA_HEADS)]
        new_m, new_acc = [], []
        ones_rows = jnp.ones((MLA_ONES_ROWS, sub), _BF)
        for h in range(MLA_HEADS):
            m_old = m_sc[h]
            if valid is None:
                m_new = jnp.maximum(m_old, smax_sc[h])
                p = jnp.exp2(s_sc[h] - m_new)
            else:
                s = jnp.where(valid, s_sc[h], -jnp.inf)
                m_new = jnp.maximum(m_old, jnp.max(s, axis=0, keepdims=True))
                p = jnp.exp2(s - m_new)
            alpha = jnp.exp2(m_old - m_new)
            new_m.append(m_new)
            p_sc[h] = p.astype(_BF)
            pv = None
            for c in range(tb // sub):
                vt = vt_ref[0, j * (tb // sub) + c, h * MLA_V_DIM:(h + 1) * MLA_V_DIM, :]
                vt1 = jnp.concatenate([vt, ones_rows], axis=0)
                part = jnp.dot(vt1, p_sc[h, c * sub:(c + 1) * sub, :], preferred_element_type=_F32)
                pv = part if pv is None else pv + part
            new_acc.append(alpha * acc_sc[h] + pv)
        for h in range(MLA_HEADS):
            m_sc[h] = new_m[h]
            acc_sc[h] = new_acc[h]
        for h, s in enumerate(s_next):
            s_sc[h] = s
            smax_sc[h] = jnp.max(s, axis=0, keepdims=True)

    def body(j, c):
        all_heads(j, None, True)
        return c

    lax.fori_loop(0, i, body, 0)
    all_heads(i, causal, False)
    out_t = jnp.concatenate(
        [acc_sc[h, :MLA_V_DIM, :] / acc_sc[h, MLA_V_DIM:MLA_V_DIM + 1, :] for h in range(MLA_HEADS)],
        axis=0)
    o_ref[...] = out_t.T


def _mla_attention(mq, mk, mvt, batch, seq):
    tb = MLA_TILE
    nblk = seq // tb
    return pl.pallas_call(
        _mla_kernel,
        grid=(batch, nblk),
        in_specs=[
            pl.BlockSpec((tb, MLA_HEADS * MLA_SLAB), lambda b, i: (b * nblk + i, 0)),
            pl.BlockSpec((tb, MLA_HEADS * MLA_SLAB),
                         lambda b, i: (b * nblk + jnp.minimum(i + 1, nblk - 1), 0)),
            pl.BlockSpec((seq, MLA_HEADS * MLA_SLAB), lambda b, i: (b, 0),
                         pipeline_mode=pl.Buffered(1)),
            pl.BlockSpec((1,) + mvt.shape[1:], lambda b, i: (b, 0, 0, 0),
                         pipeline_mode=pl.Buffered(1)),
        ],
        out_specs=pl.BlockSpec((tb, MLA_WIDTH), lambda b, i: (b * nblk + i, 0)),
        out_shape=jax.ShapeDtypeStruct((batch * seq, MLA_WIDTH), _F32),
        scratch_shapes=[pltpu.VMEM((MLA_HEADS, 1, tb), _F32),
                        pltpu.VMEM((MLA_HEADS, MLA_V_DIM + MLA_ONES_ROWS, tb), _F32),
                        pltpu.VMEM((MLA_HEADS, tb, tb), _F32),
                        pltpu.VMEM((MLA_HEADS, 1, tb), _F32),
                        pltpu.VMEM((MLA_HEADS, tb, tb), _BF)],
        compiler_params=_params("arbitrary", "arbitrary"),
        name="mla_attention",
    )(mq, mq, mk, mvt)


def _post_kernel(h_ref, sb_ref, mla_ref, p_ref, gsb_ref, gmla_ref, wout_ref,
                 gffn_ref, wg_ref, wu_ref, wd_ref, gple_ref, wpg_ref, wpp_ref, gfin_ref,
                 o_ref, *, final_norm):
    sb = _rms(sb_ref[...], gsb_ref[...]).astype(_BF)
    ml = _rms(mla_ref[...], gmla_ref[...]).astype(_BF)
    h1 = h_ref[...] + jnp.dot(sb, wout_ref[:SB_WIDTH, :], preferred_element_type=_F32)
    h1 = h1 + jnp.dot(ml, wout_ref[SB_WIDTH:, :], preferred_element_type=_F32)
    h2 = _swiglu_half_step(h1, gffn_ref, wg_ref, wu_ref, wd_ref)
    hn = _rms(h2, gple_ref[...]).astype(_BF)
    gate = jax.nn.sigmoid(jnp.dot(hn, wpg_ref[...], preferred_element_type=_F32))
    emb = jnp.dot(p_ref[...].astype(_BF), wpp_ref[...], preferred_element_type=_F32)
    out = h2 + gate * emb
    if final_norm:
        out = _rms(out, gfin_ref[...])
    o_ref[...] = out


def _post_mixer(h, sb, mla, p, li, gsb, gmla, wout, gffn, wg, wu, wd, gple, wpg, wpp, gfin,
                final_norm):
    t = h.shape[0]
    tm = POST_TILE
    row = lambda n: pl.BlockSpec((tm, n), lambda i: (i, 0))
    layered = (gsb, gmla, wout, gffn, wg, wu, wd, gple, wpg, wpp)
    return pl.pallas_call(
        functools.partial(_post_kernel, final_norm=final_norm),
        grid=(t // tm,),
        in_specs=[row(D_MODEL), row(SB_WIDTH), row(MLA_WIDTH),
                  pl.BlockSpec((None, tm, D_PLE), lambda i: (li, i, 0))]
                 + [_layer(a, li) for a in layered] + [_resident(gfin.shape)],
        out_specs=row(D_MODEL),
        out_shape=jax.ShapeDtypeStruct(h.shape, _F32),
        compiler_params=_params("parallel"),
        name="post_mixer",
    )(h, sb, mla, p, *layered, gfin)


def _swap_cols(w):
    half = w.shape[-1] // 2
    return jnp.concatenate([-w[..., half:], w[..., :half]], axis=-1)


def _mixer_weights(w_in, q_lat_norm, w_uq, kv_lat_norm, w_ukv):
    depth = w_in.shape[0]
    o1, o2, o3 = SB_WIDTH, 2 * SB_WIDTH, 3 * SB_WIDTH
    o4 = o3 + Q_LORA
    o5 = o4 + KV_LORA
    w_kr = w_in[..., o5:]
    wlat = jnp.concatenate([w_in[..., o3:o5], w_kr, _swap_cols(w_kr)], axis=-1)
    uq = w_uq.reshape(depth, Q_LORA, MLA_HEADS, MLA_QK_DIM)
    wuq = jnp.concatenate([uq, _swap_cols(uq[..., MLA_NOPE_DIM:])], axis=-1)
    ukv = w_ukv.reshape(depth, KV_LORA, MLA_HEADS, MLA_NOPE_DIM + MLA_V_DIM)
    transposed = lambda w: jnp.swapaxes(w, -1, -2).astype(_BF)
    return {
        "wq": w_in[..., :o1].astype(_BF),
        "wk": w_in[..., o1:o2].astype(_BF),
        "wvt": transposed(w_in[..., o2:o3]),
        "wlat": wlat.astype(_BF),
        "gq": q_lat_norm.reshape(depth, 1, -1),
        "gkv": kv_lat_norm.reshape(depth, 1, -1),
        "wuq": wuq.reshape(depth, Q_LORA, -1).astype(_BF),
        "wuk": ukv[..., :MLA_NOPE_DIM].reshape(depth, KV_LORA, -1).astype(_BF),
        "wuvt": transposed(ukv[..., MLA_NOPE_DIM:].reshape(depth, KV_LORA, -1)),
    }


def _rope_tables(seq):
    half = MLA_ROPE_DIM // 2
    inv_freq = ROPE_THETA ** (-np.arange(half, dtype=np.float64) / half)
    ang = np.arange(seq, dtype=np.float64)[:, None] * inv_freq[None, :]
    pad = np.zeros((seq, LANES - MLA_ROPE_DIM))
    table = lambda f: jnp.asarray(np.concatenate([f(ang), f(ang), pad], axis=1), _F32)
    return table(np.cos), table(np.sin)


def kernel(x, p, ffn1_norm, ffn1_w_gate, ffn1_w_up, ffn1_w_down, mix_norm, w_in, q_lat_norm, w_uq, kv_lat_norm, w_ukv, sb_out_norm, mla_out_norm, w_out, ffn2_norm, ffn2_w_gate, ffn2_w_up, ffn2_w_down, ple_norm, w_ple_gate, w_ple_proj, final_norm):
    batch, seq, _ = x.shape
    depth = p.shape[0]
    assert seq % MLA_TILE == 0 and seq % ATT_TILE == 0 and seq % PROJ_TILE == 0
    assert (batch * seq) % TOK_TILE == 0 and (batch * seq) % POST_TILE == 0
    t = batch * seq
    cos2, sin2 = _rope_tables(seq)
    idx = np.arange(ATT_TILE)
    tneg = jnp.asarray(np.where(idx[None, :] > idx[:, None], -1.0, 0.0), _BF)
    vec = lambda v: v.reshape(1, -1)
    gains = lambda g: g.reshape(depth, 1, -1)
    p_tok = p.reshape(depth, t, D_PLE)
    w_out_bf, w_ple_gate_bf, w_ple_proj_bf = (w.astype(_BF) for w in (w_out, w_ple_gate, w_ple_proj))
    mw = _mixer_weights(w_in, q_lat_norm, w_uq, kv_lat_norm, w_ukv)

    h = x.reshape(t, D_MODEL)
    for li in range(depth):
        h = _ffn(h, li, gains(ffn1_norm), ffn1_w_gate, ffn1_w_up, ffn1_w_down)
        sbq, sbk, sbvt, mq, mk, mvt = _proj(h, li, gains(mix_norm), mw, cos2, sin2, batch, seq)
        sb = _sb_attention(sbq, sbk, sbvt, tneg, batch, seq)
        mla = _mla_attention(mq, mk, mvt, batch, seq)
        h = _post_mixer(h, sb, mla, p_tok, li, gains(sb_out_norm), gains(mla_out_norm), w_out_bf,
                        gains(ffn2_norm), ffn2_w_gate, ffn2_w_up, ffn2_w_down,
                        gains(ple_norm), w_ple_gate_bf, w_ple_proj_bf, vec(final_norm),
                        li == depth - 1)
    return h.reshape(batch, seq, D_MODEL)
```

```python
import functools
import math

import jax
import jax.numpy as jnp
import numpy as np
from jax import lax
from jax.experimental import pallas as pl
from jax.experimental.pallas import tpu as pltpu

D_MODEL = 1024
D_PLE = 256
SB_HEADS = 8
SB_HEAD_DIM = 64
SB_WIDTH = SB_HEADS * SB_HEAD_DIM
MLA_HEADS = 4
MLA_NOPE_DIM = 128
MLA_ROPE_DIM = 64
MLA_QK_DIM = MLA_NOPE_DIM + MLA_ROPE_DIM
MLA_V_DIM = 128
MLA_WIDTH = MLA_HEADS * MLA_V_DIM
Q_LORA = 256
KV_LORA = 128
D_FF = 2816
ROPE_THETA = 10000.0
EPS = 1e-6

LANES = 128
MXU_DIM = 256
VMEM_LIMIT_BYTES = 58 * 1024 * 1024

FF_CHUNK = MXU_DIM
N_FF_CHUNKS = D_FF // FF_CHUNK
TOK_TILE = 512
POST_TILE = 512
PROJ_TILE = 1024
ATT_TILE = 256
MLA_TILE = 512
MLA_SLAB = 2 * LANES
MLA_ONES_ROWS = 16
SB_DEAD_LOG2 = -152.0
SB_MASKED_LOGIT = -1e30

_BF = jnp.bfloat16
_F32 = jnp.float32
_NT = (((1,), (1,)), ((), ()))


def _rms(x, g):
    return x * lax.rsqrt(jnp.mean(x * x, axis=-1, keepdims=True) + EPS) * g


def _params(*sem):
    return pltpu.CompilerParams(dimension_semantics=sem, vmem_limit_bytes=VMEM_LIMIT_BYTES)


def _resident(shape):
    zeros = (0,) * len(shape)
    return pl.BlockSpec(shape, lambda *_: zeros, pipeline_mode=pl.Buffered(1))


def _layer(arr, li):
    index = (li,) + (0,) * (arr.ndim - 1)
    return pl.BlockSpec((None,) + arr.shape[1:], lambda *_: index, pipeline_mode=pl.Buffered(1))


def _swiglu_half_step(x, g_ref, wg_ref, wu_ref, wd_ref):
    xn = _rms(x, g_ref[...]).astype(_BF)
    acc = jnp.zeros(x.shape, _F32)
    for c in range(N_FF_CHUNKS):
        cols = slice(c * FF_CHUNK, (c + 1) * FF_CHUNK)
        gate = jnp.dot(xn, wg_ref[:, cols].astype(_BF), preferred_element_type=_F32)
        up = jnp.dot(xn, wu_ref[:, cols].astype(_BF), preferred_element_type=_F32)
        hid = (gate * jax.nn.sigmoid(gate) * up).astype(_BF)
        acc = acc + jnp.dot(hid, wd_ref[cols, :].astype(_BF), preferred_element_type=_F32)
    return x + 0.5 * acc


def _ffn_kernel(x_ref, g_ref, wg_ref, wu_ref, wd_ref, o_ref):
    o_ref[...] = _swiglu_half_step(x_ref[...], g_ref, wg_ref, wu_ref, wd_ref)


def _ffn(x, li, g, wg, wu, wd):
    t = x.shape[0]
    row = pl.BlockSpec((TOK_TILE, D_MODEL), lambda i: (i, 0))
    return pl.pallas_call(
        _ffn_kernel,
        grid=(t // TOK_TILE,),
        in_specs=[row, _layer(g, li), _layer(wg, li), _layer(wu, li), _layer(wd, li)],
        out_specs=row,
        out_shape=jax.ShapeDtypeStruct(x.shape, _F32),
        compiler_params=_params("parallel"),
        name="ffn",
    )(x, g, wg, wu, wd)


def _rope_slab(x, cos, sin):
    return x * cos + pltpu.roll(x, LANES // 2, axis=1) * sin


def _proj_kernel(h_ref, g_ref, wq_ref, wk_ref, wvt_ref, wlat_ref, gq_ref, gkv_ref,
                 wuq_ref, wuk_ref, wuvt_ref, cos_ref, sin_ref,
                 sbq_ref, sbk_ref, sbvt_ref, mq_ref, mk_ref, mvt_ref):
    u = _rms(h_ref[...], g_ref[...]).astype(_BF)
    sb_scale = SB_HEAD_DIM ** -0.5 * math.log2(math.e)
    sbq_ref[...] = (jnp.dot(u, wq_ref[...], preferred_element_type=_F32) * sb_scale).astype(_BF)
    sbk_ref[...] = jnp.dot(u, wk_ref[...], preferred_element_type=_F32).astype(_BF)
    sbvt = lax.dot_general(wvt_ref[...], u, _NT, preferred_element_type=_F32).astype(_BF)
    for c in range(sbvt_ref.shape[1]):
        sbvt_ref[0, c] = sbvt[:, c * ATT_TILE:(c + 1) * ATT_TILE]

    lat = jnp.dot(u, wlat_ref[...], preferred_element_type=_F32)
    cos = cos_ref[...]
    sin = sin_ref[...]
    cq = _rms(lat[:, :Q_LORA], gq_ref[...]).astype(_BF)
    ckv = _rms(lat[:, Q_LORA:Q_LORA + KV_LORA], gkv_ref[...]).astype(_BF)
    o_kr = Q_LORA + KV_LORA
    k_rope = _rope_slab(lat[:, o_kr:o_kr + LANES], cos, sin).astype(_BF)

    qa = jnp.dot(cq, wuq_ref[...], preferred_element_type=_F32)
    kn = jnp.dot(ckv, wuk_ref[...], preferred_element_type=_F32)
    mvt = lax.dot_general(wuvt_ref[...], ckv, _NT, preferred_element_type=_F32).astype(_BF)
    for c in range(mvt_ref.shape[1]):
        mvt_ref[0, c] = mvt[:, c * ATT_TILE:(c + 1) * ATT_TILE]
    scale = MLA_QK_DIM ** -0.5 * math.log2(math.e)
    for hd in range(MLA_HEADS):
        o = hd * MLA_SLAB
        mq_ref[:, o:o + LANES] = (qa[:, o:o + LANES] * scale).astype(_BF)
        roped = _rope_slab(qa[:, o + LANES:o + 2 * LANES], cos, sin)
        mq_ref[:, o + LANES:o + 2 * LANES] = (roped * scale).astype(_BF)
        mk_ref[:, o:o + LANES] = kn[:, hd * LANES:(hd + 1) * LANES].astype(_BF)
        mk_ref[:, o + LANES:o + 2 * LANES] = k_rope


def _proj(h, li, g, w, cos2, sin2, batch, seq):
    t = batch * seq
    tm = PROJ_TILE
    nblk = seq // tm
    kb = ATT_TILE
    row = lambda n: pl.BlockSpec((tm, n), lambda i: (i, 0))
    outs = (
        jax.ShapeDtypeStruct((t, SB_WIDTH), _BF),
        jax.ShapeDtypeStruct((t, SB_WIDTH), _BF),
        jax.ShapeDtypeStruct((batch, seq // kb, SB_WIDTH, kb), _BF),
        jax.ShapeDtypeStruct((t, MLA_HEADS * MLA_SLAB), _BF),
        jax.ShapeDtypeStruct((t, MLA_HEADS * MLA_SLAB), _BF),
        jax.ShapeDtypeStruct((batch, seq // kb, MLA_WIDTH, kb), _BF),
    )
    weights = (g, w["wq"], w["wk"], w["wvt"], w["wlat"], w["gq"], w["gkv"],
               w["wuq"], w["wuk"], w["wuvt"])
    vt_spec = pl.BlockSpec((1, tm // kb, SB_WIDTH, kb), lambda i: (i // nblk, i % nblk, 0, 0))
    rope_spec = pl.BlockSpec((tm, LANES), lambda i: (i % nblk, 0))
    return pl.pallas_call(
        _proj_kernel,
        grid=(t // tm,),
        in_specs=[row(D_MODEL)] + [_layer(a, li) for a in weights] + [rope_spec, rope_spec],
        out_specs=(row(SB_WIDTH), row(SB_WIDTH), vt_spec,
                   row(MLA_HEADS * MLA_SLAB), row(MLA_HEADS * MLA_SLAB), vt_spec),
        out_shape=outs,
        compiler_params=_params("parallel"),
        name="mixer_proj",
    )(h, *weights, cos2, sin2)


def _sb_kernel(q_ref, k_ref, vt_ref, tneg_ref, o_ref, carry_sc, acc_sc, spb_sc, lsig_sc, ab_sc):
    tb = ATT_TILE
    hd = SB_HEAD_DIM
    i = pl.program_id(1)
    lane = lax.broadcasted_iota(jnp.int32, (tb, LANES), 1)
    tneg = tneg_ref[...]
    key_row = lax.broadcasted_iota(jnp.int32, (tb, SB_HEADS * tb), 0)
    qry_col = lax.broadcasted_iota(jnp.int32, (tb, SB_HEADS * tb), 1) & (tb - 1)
    valid = key_row < qry_col

    carry_sc[...] = jnp.zeros(carry_sc.shape, _F32)
    acc_sc[...] = jnp.zeros(acc_sc.shape, _F32)

    q_pairs = []
    for pair in range(SB_HEADS // 2):
        q = q_ref[:, pair * LANES:(pair + 1) * LANES]
        zero = jnp.zeros_like(q)
        q_pairs.append(jnp.concatenate([jnp.where(lane < hd, q, zero),
                                        jnp.where(lane >= hd, q, zero)], axis=0))

    def key_blocks(blocks):
        operands = []
        for j, _ in blocks:
            rows = pl.ds(pl.multiple_of(j * tb, tb), tb)
            operands.append((
                [k_ref[rows, pair * LANES:(pair + 1) * LANES] for pair in range(SB_HEADS // 2)],
                [vt_ref[0, j, h * hd:(h + 1) * hd, :] for h in range(SB_HEADS)]))
        carry = carry_sc[...]
        total = None
        for slot, ((_, mask), (kbs, vbs)) in enumerate(zip(blocks, operands)):
            z = jnp.concatenate([lax.dot_general(kb, qp, _NT, preferred_element_type=_F32)
                                 for kb, qp in zip(kbs, q_pairs)], axis=1)
            if mask is not None:
                z = jnp.where(mask, z, SB_MASKED_LOGIT)
            sp = jnp.maximum(z, 0.0) + jnp.log2(1.0 + jnp.exp2(-jnp.abs(z)))
            spb_sc[slot] = sp.astype(_BF)
            lsig_sc[slot] = z - sp
            tail = jnp.dot(tneg, spb_sc[slot], preferred_element_type=_F32)
            ab_sc[slot] = jnp.exp2(lsig_sc[slot] + tail + carry).astype(_BF)
            contrib = jnp.concatenate(
                [jnp.dot(vbs[h], ab_sc[slot, :, h * tb:(h + 1) * tb], preferred_element_type=_F32)
                 for h in range(SB_HEADS)], axis=0)
            carry = carry + tail[0:1, :] - sp[0:1, :]
            total = contrib if total is None else total + contrib
        carry_sc[...] = carry
        acc_sc[...] += total

    def alive():
        return (jnp.max(carry_sc[...]) > SB_DEAD_LOG2).astype(jnp.int32)

    @pl.when(i == 0)
    def _():
        key_blocks([(i, valid)])

    @pl.when(i > 0)
    def _():
        key_blocks([(i, valid), (i - 1, None)])

    def cond(state):
        j, live = state
        return jnp.logical_and(j >= 0, live > 0)

    def body(state):
        j, _ = state
        key_blocks([(j, None)])
        return j - 1, alive()

    lax.while_loop(cond, body, (i - 2, alive()))
    o_ref[...] = acc_sc[...].T


def _sb_attention(sbq, sbk, sbvt, tneg, batch, seq):
    tb = ATT_TILE
    nblk = seq // tb
    return pl.pallas_call(
        _sb_kernel,
        grid=(batch, nblk),
        in_specs=[
            pl.BlockSpec((tb, SB_WIDTH), lambda b, i: (b * nblk + i, 0)),
            pl.BlockSpec((seq, SB_WIDTH), lambda b, i: (b, 0), pipeline_mode=pl.Buffered(1)),
            pl.BlockSpec((1, nblk, SB_WIDTH, tb), lambda b, i: (b, 0, 0, 0),
                         pipeline_mode=pl.Buffered(1)),
            _resident(tneg.shape),
        ],
        out_specs=pl.BlockSpec((tb, SB_WIDTH), lambda b, i: (b * nblk + i, 0)),
        out_shape=jax.ShapeDtypeStruct((batch * seq, SB_WIDTH), _F32),
        scratch_shapes=[pltpu.VMEM((1, SB_HEADS * tb), _F32), pltpu.VMEM((SB_WIDTH, tb), _F32),
                        pltpu.VMEM((2, tb, SB_HEADS * tb), _BF),
                        pltpu.VMEM((2, tb, SB_HEADS * tb), _F32),
                        pltpu.VMEM((2, tb, SB_HEADS * tb), _BF)],
        compiler_params=_params("parallel", "arbitrary"),
        name="sb_attention",
    )(sbq, sbk, sbvt, tneg)


def _mla_kernel(q_ref, qnext_ref, k_ref, vt_ref, o_ref, m_sc, acc_sc, s_sc, smax_sc, p_sc):
    tb = MLA_TILE
    sub = ATT_TILE
    i = pl.program_id(1)
    m_sc[...] = jnp.full(m_sc.shape, -1e30, _F32)
    acc_sc[...] = jnp.zeros(acc_sc.shape, _F32)
    key_row = lax.broadcasted_iota(jnp.int32, (tb, tb), 0)
    qry_col = lax.broadcasted_iota(jnp.int32, (tb, tb), 1)
    causal = key_row <= qry_col

    def scores(j, h):
        rows = pl.ds(pl.multiple_of(j * tb, tb), tb)
        kb = k_ref[rows, h * MLA_SLAB:(h + 1) * MLA_SLAB]
        q = q_ref[:, h * MLA_SLAB:(h + 1) * MLA_SLAB]
        return lax.dot_general(kb, q, _NT, preferred_element_type=_F32)

    def first_scores(query_ref, h):
        kb = k_ref[0:tb, h * MLA_SLAB:(h + 1) * MLA_SLAB]
        q = query_ref[:, h * MLA_SLAB:(h + 1) * MLA_SLAB]
        return lax.dot_general(kb, q, _NT, preferred_element_type=_F32)

    @pl.when(i == 0)
    def _():
        for h in range(MLA_HEADS):
            s = first_scores(q_ref, h)
            s_sc[h] = s
            smax_sc[h] = jnp.max(s, axis=0, keepdims=True)

    def all_heads(j, valid, prefetch_next):
        if prefetch_next:
            s_next = [scores(j + 1, h) for h in range(MLA_HEADS)]
        else:
            s_next = [first_scores(qnext_ref, h) for h in range(MLA_HEADS)]
        new_m, new_acc = [], []
        ones_rows = jnp.ones((MLA_ONES_ROWS, sub), _BF)
        for h in range(MLA_HEADS):
            m_old = m_sc[h]
            first_q = [0 if valid is None else c * sub for c in range(tb // sub)]
            if valid is None:
                m_new = jnp.maximum(m_old, smax_sc[h])
                p_sc[h] = jnp.exp2(s_sc[h] - m_new).astype(_BF)
            else:
                parts, m_new = [], m_old
                for c, q0 in enumerate(first_q):
                    keys = slice(c * sub, (c + 1) * sub)
                    s = jnp.where(valid[keys, q0:], s_sc[h, keys, q0:], -jnp.inf)
                    parts.append(s)
                    top = jnp.max(s, axis=0, keepdims=True)
                    if q0:
                        top = jnp.concatenate([jnp.full((1, q0), -jnp.inf, _F32), top], axis=1)
                    m_new = jnp.maximum(m_new, top)
            alpha = jnp.exp2(m_old - m_new)
            new_m.append(m_new)
            pv = None
            for c, q0 in enumerate(first_q):
                keys = slice(c * sub, (c + 1) * sub)
                if valid is not None:
                    p_sc[h, keys, q0:] = jnp.exp2(parts[c] - m_new[:, q0:]).astype(_BF)
                vt = vt_ref[0, j * (tb // sub) + c, h * MLA_V_DIM:(h + 1) * MLA_V_DIM, :]
                vt1 = jnp.concatenate([vt, ones_rows], axis=0)
                part = jnp.dot(vt1, p_sc[h, keys, q0:], preferred_element_type=_F32)
                if q0:
                    part = jnp.concatenate([jnp.zeros((part.shape[0], q0), _F32), part], axis=1)
                pv = part if pv is None else pv + part
            new_acc.append(alpha * acc_sc[h] + pv)
        for h in range(MLA_HEADS):
            m_sc[h] = new_m[h]
            acc_sc[h] = new_acc[h]
        for h, s in enumerate(s_next):
            s_sc[h] = s
            smax_sc[h] = jnp.max(s, axis=0, keepdims=True)

    def body(j, c):
        all_heads(j, None, True)
        return c

    lax.fori_loop(0, i, body, 0)
    all_heads(i, causal, False)
    out_t = jnp.concatenate(
        [acc_sc[h, :MLA_V_DIM, :] / acc_sc[h, MLA_V_DIM:MLA_V_DIM + 1, :] for h in range(MLA_HEADS)],
        axis=0)
    o_ref[...] = out_t.T


def _mla_attention(mq, mk, mvt, batch, seq):
    tb = MLA_TILE
    nblk = seq // tb
    return pl.pallas_call(
        _mla_kernel,
        grid=(batch, nblk),
        in_specs=[
            pl.BlockSpec((tb, MLA_HEADS * MLA_SLAB), lambda b, i: (b * nblk + i, 0)),
            pl.BlockSpec((tb, MLA_HEADS * MLA_SLAB),
                         lambda b, i: (b * nblk + jnp.minimum(i + 1, nblk - 1), 0)),
            pl.BlockSpec((seq, MLA_HEADS * MLA_SLAB), lambda b, i: (b, 0),
                         pipeline_mode=pl.Buffered(1)),
            pl.BlockSpec((1,) + mvt.shape[1:], lambda b, i: (b, 0, 0, 0),
                         pipeline_mode=pl.Buffered(1)),
        ],
        out_specs=pl.BlockSpec((tb, MLA_WIDTH), lambda b, i: (b * nblk + i, 0)),
        out_shape=jax.ShapeDtypeStruct((batch * seq, MLA_WIDTH), _F32),
        scratch_shapes=[pltpu.VMEM((MLA_HEADS, 1, tb), _F32),
                        pltpu.VMEM((MLA_HEADS, MLA_V_DIM + MLA_ONES_ROWS, tb), _F32),
                        pltpu.VMEM((MLA_HEADS, tb, tb), _F32),
                        pltpu.VMEM((MLA_HEADS, 1, tb), _F32),
                        pltpu.VMEM((MLA_HEADS, tb, tb), _BF)],
        compiler_params=_params("arbitrary", "arbitrary"),
        name="mla_attention",
    )(mq, mq, mk, mvt)


def _post_kernel(h_ref, sb_ref, mla_ref, p_ref, gsb_ref, gmla_ref, wout_ref,
                 gffn_ref, wg_ref, wu_ref, wd_ref, gple_ref, wpg_ref, wpp_ref, gfin_ref,
                 o_ref, *, final_norm):
    sb = _rms(sb_ref[...], gsb_ref[...]).astype(_BF)
    ml = _rms(mla_ref[...], gmla_ref[...]).astype(_BF)
    h1 = h_ref[...] + jnp.dot(sb, wout_ref[:SB_WIDTH, :], preferred_element_type=_F32)
    h1 = h1 + jnp.dot(ml, wout_ref[SB_WIDTH:, :], preferred_element_type=_F32)
    h2 = _swiglu_half_step(h1, gffn_ref, wg_ref, wu_ref, wd_ref)
    hn = _rms(h2, gple_ref[...]).astype(_BF)
    gate = jax.nn.sigmoid(jnp.dot(hn, wpg_ref[...], preferred_element_type=_F32))
    emb = jnp.dot(p_ref[...].astype(_BF), wpp_ref[...], preferred_element_type=_F32)
    out = h2 + gate * emb
    if final_norm:
        out = _rms(out, gfin_ref[...])
    o_ref[...] = out


def _post_mixer(h, sb, mla, p, li, gsb, gmla, wout, gffn, wg, wu, wd, gple, wpg, wpp, gfin,
                final_norm):
    t = h.shape[0]
    tm = POST_TILE
    row = lambda n: pl.BlockSpec((tm, n), lambda i: (i, 0))
    layered = (gsb, gmla, wout, gffn, wg, wu, wd, gple, wpg, wpp)
    return pl.pallas_call(
        functools.partial(_post_kernel, final_norm=final_norm),
        grid=(t // tm,),
        in_specs=[row(D_MODEL), row(SB_WIDTH), row(MLA_WIDTH),
                  pl.BlockSpec((None, tm, D_PLE), lambda i: (li, i, 0))]
                 + [_layer(a, li) for a in layered] + [_resident(gfin.shape)],
        out_specs=row(D_MODEL),
        out_shape=jax.ShapeDtypeStruct(h.shape, _F32),
        compiler_params=_params("parallel"),
        name="post_mixer",
    )(h, sb, mla, p, *layered, gfin)


def _swap_cols(w):
    half = w.shape[-1] // 2
    return jnp.concatenate([-w[..., half:], w[..., :half]], axis=-1)


def _mixer_weights(w_in, q_lat_norm, w_uq, kv_lat_norm, w_ukv):
    depth = w_in.shape[0]
    o1, o2, o3 = SB_WIDTH, 2 * SB_WIDTH, 3 * SB_WIDTH
    o4 = o3 + Q_LORA
    o5 = o4 + KV_LORA
    w_kr = w_in[..., o5:]
    wlat = jnp.concatenate([w_in[..., o3:o5], w_kr, _swap_cols(w_kr)], axis=-1)
    uq = w_uq.reshape(depth, Q_LORA, MLA_HEADS, MLA_QK_DIM)
    wuq = jnp.concatenate([uq, _swap_cols(uq[..., MLA_NOPE_DIM:])], axis=-1)
    ukv = w_ukv.reshape(depth, KV_LORA, MLA_HEADS, MLA_NOPE_DIM + MLA_V_DIM)
    transposed = lambda w: jnp.swapaxes(w, -1, -2).astype(_BF)
    return {
        "wq": w_in[..., :o1].astype(_BF),
        "wk": w_in[..., o1:o2].astype(_BF),
        "wvt": transposed(w_in[..., o2:o3]),
        "wlat": wlat.astype(_BF),
        "gq": q_lat_norm.reshape(depth, 1, -1),
        "gkv": kv_lat_norm.reshape(depth, 1, -1),
        "wuq": wuq.reshape(depth, Q_LORA, -1).astype(_BF),
        "wuk": ukv[..., :MLA_NOPE_DIM].reshape(depth, KV_LORA, -1).astype(_BF),
        "wuvt": transposed(ukv[..., MLA_NOPE_DIM:].reshape(depth, KV_LORA, -1)),
    }


def _rope_tables(seq):
    half = MLA_ROPE_DIM // 2
    inv_freq = ROPE_THETA ** (-np.arange(half, dtype=np.float64) / half)
    ang = np.arange(seq, dtype=np.float64)[:, None] * inv_freq[None, :]
    pad = np.zeros((seq, LANES - MLA_ROPE_DIM))
    table = lambda f: jnp.asarray(np.concatenate([f(ang), f(ang), pad], axis=1), _F32)
    return table(np.cos), table(np.sin)


def kernel(x, p, ffn1_norm, ffn1_w_gate, ffn1_w_up, ffn1_w_down, mix_norm, w_in, q_lat_norm, w_uq, kv_lat_norm, w_ukv, sb_out_norm, mla_out_norm, w_out, ffn2_norm, ffn2_w_gate, ffn2_w_up, ffn2_w_down, ple_norm, w_ple_gate, w_ple_proj, final_norm):
    batch, seq, _ = x.shape
    depth = p.shape[0]
    assert seq % MLA_TILE == 0 and seq % ATT_TILE == 0 and seq % PROJ_TILE == 0
    assert (batch * seq) % TOK_TILE == 0 and (batch * seq) % POST_TILE == 0
    t = batch * seq
    cos2, sin2 = _rope_tables(seq)
    idx = np.arange(ATT_TILE)
    tneg = jnp.asarray(np.where(idx[None, :] > idx[:, None], -1.0, 0.0), _BF)
    vec = lambda v: v.reshape(1, -1)
    gains = lambda g: g.reshape(depth, 1, -1)
    p_tok = p.reshape(depth, t, D_PLE)
    w_out_bf, w_ple_gate_bf, w_ple_proj_bf = (w.astype(_BF) for w in (w_out, w_ple_gate, w_ple_proj))
    mw = _mixer_weights(w_in, q_lat_norm, w_uq, kv_lat_norm, w_ukv)

    h = x.reshape(t, D_MODEL)
    for li in range(depth):
        h = _ffn(h, li, gains(ffn1_norm), ffn1_w_gate, ffn1_w_up, ffn1_w_down)
        sbq, sbk, sbvt, mq, mk, mvt = _proj(h, li, gains(mix_norm), mw, cos2, sin2, batch, seq)
        sb = _sb_attention(sbq, sbk, sbvt, tneg, batch, seq)
        mla = _mla_attention(mq, mk, mvt, batch, seq)
        h = _post_mixer(h, sb, mla, p_tok, li, gains(sb_out_norm), gains(mla_out_norm), w_out_bf,
                        gains(ffn2_norm), ffn2_w_gate, ffn2_w_up, ffn2_w_down,
                        gains(ple_norm), w_ple_gate_bf, w_ple_proj_bf, vec(final_norm),
                        li == depth - 1)
    return h.reshape(batch, seq, D_MODEL)
```
